```python
import math, functools
import jax, jax.numpy as jnp
from jax import lax
import numpy as np

D_MODEL = 2048
BATCH = 2
SEQ = 4096
DEPTH = 4
DEC_BATCH = 8
DEC_SEQ = 8
PAST_LEN = 16384
PAGE_SIZE = 128

GLA_WIDTH = D_MODEL // 2
GLA_HEADS = 4
GLA_DV = GLA_WIDTH // GLA_HEADS
GLA_DK = GLA_DV // 2
GLA_QK = GLA_HEADS * GLA_DK
GLA_GATE_RANK = 16
GLA_TAU = 16.0
GLA_CHUNK = 64
SB_WIDTH = D_MODEL - GLA_WIDTH
SB_HD = 128
SB_HEADS = SB_WIDTH // SB_HD
SB_BLOCK = 128
SB_LOGIT_OFFSET = 5.0
N_EXPERTS = 32
TOP_K = 4
D_EXPERT = D_MODEL
SWIGLU_LIMIT = 7.0
SWIGLU_ALPHA = 1.702
MOE_BLOCK = 128
LN_EPS = 1e-5
DEEPNORM_ALPHA = (2 * DEPTH) ** 0.25
DEEPNORM_BETA = (8 * DEPTH) ** -0.25
COL_SIZES = (GLA_QK, GLA_QK, GLA_WIDTH, GLA_WIDTH, GLA_GATE_RANK, SB_WIDTH, SB_WIDTH, SB_WIDTH)
IN_COLS = 2 * GLA_QK + 2 * GLA_WIDTH + GLA_GATE_RANK + 3 * SB_WIDTH

kernel_name = 'hymba_gla_stickbreaking_moe_deepnorm_step'


def layer_norm(x, w, b):
    xf = x.astype(jnp.float32)
    mu = jnp.mean(xf, axis=-1, keepdims=True)
    var = jnp.mean(jnp.square(xf - mu), axis=-1, keepdims=True)
    return ((xf - mu) * lax.rsqrt(var + LN_EPS) * w + b).astype(x.dtype)


def head_layer_norm(o, w):
    mu = jnp.mean(o, axis=-1, keepdims=True)
    var = jnp.mean(jnp.square(o - mu), axis=-1, keepdims=True)
    return (o - mu) * lax.rsqrt(var + LN_EPS) * w.astype(jnp.float32).reshape(GLA_HEADS, GLA_DV)


def gla_chunked(q, k, v, log_a, s0):
    B, T, H, _ = q.shape
    C = math.gcd(GLA_CHUNK, T)
    nc = T // C

    def to_chunks(a):
        return a.reshape(B, nc, C, H, a.shape[-1]).transpose(1, 0, 3, 2, 4).astype(jnp.float32)

    causal = jnp.tril(jnp.ones((C, C), dtype=bool))[:, :, None]

    def step(S, inp):
        qc, kc, vc, gc = inp
        b = jnp.cumsum(gc, axis=2)
        o_inter = jnp.einsum('bhtd,bhde->bhte', qc * jnp.exp(b), S)
        diff = b[:, :, :, None, :] - b[:, :, None, :, :]
        decay = jnp.exp(jnp.where(causal, diff, -jnp.inf))
        scores = jnp.einsum('bhtd,bhsd,bhtsd->bhts', qc, kc, decay)
        o_intra = jnp.einsum('bhts,bhse->bhte', scores, vc)
        b_last = b[:, :, -1:, :]
        S_new = jnp.exp(b_last[:, :, 0, :])[..., None] * S + jnp.einsum(
            'bhsd,bhse->bhde', kc * jnp.exp(b_last - b), vc)
        return S_new, o_inter + o_intra

    S_fin, o = lax.scan(step, s0.astype(jnp.float32),
                        (to_chunks(q), to_chunks(k), to_chunks(v), to_chunks(log_a)))
    o = o.transpose(1, 0, 3, 2, 4).reshape(B, T, H, v.shape[-1])
    return o, S_fin


def stick_breaking(q, k, v, q_pos, k_pos):
    z = jnp.einsum('bqhd,bkhd->bhqk', q.astype(jnp.float32), k.astype(jnp.float32)) * (SB_HD ** -0.5)
    live = k_pos[None, :] < q_pos[:, None]
    sp = jnp.where(live, jax.nn.softplus(z), 0.0)
    rev = lax.cumsum(sp, axis=3, reverse=True)
    w = jnp.exp(jnp.where(live, z - rev, -jnp.inf))
    return jnp.einsum('bhqk,bkhd->bqhd', w, v.astype(jnp.float32))


def sb_prompt(q, k, v):
    B, T, H, d = q.shape
    nb = T // SB_BLOCK
    pos = jnp.arange(T, dtype=jnp.int32)
    qb = q.reshape(B, nb, SB_BLOCK, H, d).transpose(1, 0, 2, 3, 4)
    out = lax.map(lambda a: stick_breaking(a[0], k, v, a[1], pos), (qb, pos.reshape(nb, SB_BLOCK)))
    return out.transpose(1, 0, 2, 3, 4).reshape(B, T, H, d)


def sb_sample(q, k, v, k_past, v_past, q_pos, k_pos):
    k_all = jnp.concatenate([k_past.astype(k.dtype), k], axis=1)
    v_all = jnp.concatenate([v_past.astype(v.dtype), v], axis=1)
    return stick_breaking(q, k_all, v_all, q_pos, k_pos)


def token_mixer(x, w_in, b_sb_qk, w_a2, b_a, norm_w, w_o, s0, sb_attend):
    B, T, _ = x.shape
    h = x @ w_in
    splits, acc = [], 0
    for c in COL_SIZES[:-1]:
        acc += c
        splits.append(acc)
    q_g, k_g, v_g, r_g, a_lr, q_s, k_s, v_s = jnp.split(h, splits, axis=-1)
    log_a = jax.nn.log_sigmoid((a_lr @ w_a2 + b_a).astype(jnp.float32)) / GLA_TAU
    qg = q_g.reshape(B, T, GLA_HEADS, GLA_DK) * (GLA_DK ** -0.5)
    kg = k_g.reshape(B, T, GLA_HEADS, GLA_DK)
    vg = v_g.reshape(B, T, GLA_HEADS, GLA_DV)
    o_g, s_fin = gla_chunked(qg, kg, vg, log_a.reshape(B, T, GLA_HEADS, GLA_DK), s0)
    o_g = head_layer_norm(o_g, norm_w).reshape(B, T, GLA_WIDTH) * jax.nn.silu(r_g.astype(jnp.float32))
    qs = (q_s + b_sb_qk[:SB_WIDTH]).reshape(B, T, SB_HEADS, SB_HD)
    ks = (k_s + b_sb_qk[SB_WIDTH:]).reshape(B, T, SB_HEADS, SB_HD)
    vs = v_s.reshape(B, T, SB_HEADS, SB_HD)
    o_s = sb_attend(qs, ks, vs).reshape(B, T, SB_WIDTH)
    mix = jnp.concatenate([o_g, o_s], axis=-1).astype(x.dtype) @ w_o
    return mix, ks, vs, s_fin


def moe(x, w_router, b_router, w_up, b_up, w_down, b_down):
    B, T, D = x.shape
    xt = x.reshape(-1, D)
    n_tok = xt.shape[0]
    logits = (xt @ w_router + b_router).astype(jnp.float32)
    top_val, top_idx = lax.top_k(logits, TOP_K)
    gates = jax.nn.softmax(top_val, axis=-1)
    n_asg = n_tok * TOP_K
    flat_e = top_idx.reshape(-1)
    order = jnp.argsort(flat_e)
    sorted_e = flat_e[order]
    counts = jnp.zeros((N_EXPERTS,), jnp.int32).at[flat_e].add(1)
    starts = jnp.cumsum(counts) - counts
    padded = (counts + MOE_BLOCK - 1) // MOE_BLOCK * MOE_BLOCK
    pad_ends = jnp.cumsum(padded)
    pad_starts = pad_ends - padded
    dest = pad_starts[sorted_e] + (jnp.arange(n_asg, dtype=jnp.int32) - starts[sorted_e])
    n_blocks = -(-n_asg // MOE_BLOCK) + N_EXPERTS
    n_slots = n_blocks * MOE_BLOCK
    slot_tok = jnp.full((n_slots,), n_tok, jnp.int32).at[dest].set(order // TOP_K)
    slot_gate = jnp.zeros((n_slots,), jnp.float32).at[dest].set(gates.reshape(-1)[order])
    block_e = jnp.minimum(jnp.searchsorted(pad_ends, jnp.arange(n_blocks, dtype=jnp.int32) * MOE_BLOCK,
                                           side='right'), N_EXPERTS - 1)
    x_pad = jnp.concatenate([xt, jnp.zeros((1, D), xt.dtype)], axis=0)
    xb = x_pad[slot_tok].reshape(n_blocks, MOE_BLOCK, D)

    def expert_block(a):
        xblk, e = a
        hh = xblk @ w_up[e] + b_up[e]
        g = jnp.minimum(hh[:, :D_EXPERT], SWIGLU_LIMIT)
        u = jnp.clip(hh[:, D_EXPERT:], -SWIGLU_LIMIT, SWIGLU_LIMIT)
        glu = g * jax.nn.sigmoid(SWIGLU_ALPHA * g)
        return ((u + 1.0) * glu) @ w_down[e] + b_down[e]

    yb = lax.map(expert_block, (xb, block_e))
    y = yb.reshape(n_slots, D).astype(jnp.float32) * slot_gate[:, None]
    out = jax.ops.segment_sum(y, slot_tok, num_segments=n_tok + 1)[:n_tok]
    return out.reshape(B, T, D).astype(x.dtype)


def setup_inputs(seed: int = 0) -> dict:
    key = jax.random.key(seed)
    ks = jax.random.split(key, 24)
    f32 = jnp.float32
    n_pages = PAST_LEN // PAGE_SIZE
    n_used = DEC_BATCH * n_pages
    n_pool = n_used + max(1, n_used // 4)
    col_scale = np.concatenate([np.full((c,), s, np.float32) for c, s in zip(
        COL_SIZES, (1.0, 1.0, DEEPNORM_BETA, 1.0, 1.0, 1.0, 1.0, DEEPNORM_BETA))])
    nrm = lambda k, shp: jax.random.normal(k, shp, f32)
    sb_dir = nrm(ks[21], (DEPTH, SB_HEADS, SB_HD))
    sb_dir = sb_dir / jnp.linalg.norm(sb_dir, axis=-1, keepdims=True)
    sb_c = math.sqrt(SB_LOGIT_OFFSET * math.sqrt(SB_HD))
    b_q = sb_c * sb_dir + 0.05 * nrm(ks[22], (DEPTH, SB_HEADS, SB_HD))
    b_k = -sb_c * sb_dir + 0.05 * nrm(ks[23], (DEPTH, SB_HEADS, SB_HD))
    b_sb_qk = jnp.concatenate([b_q.reshape(DEPTH, SB_WIDTH), b_k.reshape(DEPTH, SB_WIDTH)], axis=-1)
    x_prompt = nrm(ks[0], (BATCH, SEQ, D_MODEL))
    x_sample = nrm(ks[1], (DEC_BATCH, DEC_SEQ, D_MODEL))
    cache_sb_k = nrm(ks[2], (DEPTH, n_pool, PAGE_SIZE, SB_HEADS, SB_HD)) + b_k[:, None, None]
    cache_sb_v = nrm(ks[3], (DEPTH, n_pool, PAGE_SIZE, SB_HEADS, SB_HD)) * DEEPNORM_BETA
    state_gla = nrm(ks[4], (DEPTH, DEC_BATCH, GLA_HEADS, GLA_DK, GLA_DV)) * 0.5
    page_table = jax.random.permutation(ks[5], n_pool)[:n_used].reshape(DEC_BATCH, n_pages).astype(jnp.int32)
    w_in = nrm(ks[6], (DEPTH, D_MODEL, IN_COLS)) * (D_MODEL ** -0.5) * jnp.asarray(col_scale)
    w_gla_a2 = nrm(ks[7], (DEPTH, GLA_GATE_RANK, GLA_QK)) * (GLA_GATE_RANK ** -0.5)
    b_gla_a = nrm(ks[8], (DEPTH, GLA_QK)) * 0.1
    gla_norm_w = 1.0 + 0.02 * nrm(ks[9], (DEPTH, GLA_WIDTH))
    w_o = nrm(ks[10], (DEPTH, D_MODEL, D_MODEL)) * (D_MODEL ** -0.5) * DEEPNORM_BETA
    ln1_w = 1.0 + 0.02 * nrm(ks[11], (DEPTH, D_MODEL))
    ln1_b = 0.02 * nrm(ks[12], (DEPTH, D_MODEL))
    w_router = nrm(ks[13], (DEPTH, D_MODEL, N_EXPERTS)) * (D_MODEL ** -0.5)
    b_router = 0.01 * nrm(ks[14], (DEPTH, N_EXPERTS))
    w_up = nrm(ks[15], (DEPTH, N_EXPERTS, D_MODEL, 2 * D_EXPERT)) * (D_MODEL ** -0.5)
    b_up = 0.01 * nrm(ks[16], (DEPTH, N_EXPERTS, 2 * D_EXPERT))
    w_down = nrm(ks[17], (DEPTH, N_EXPERTS, D_EXPERT, D_MODEL)) * (D_EXPERT ** -0.5) * DEEPNORM_BETA
    b_down = 0.01 * nrm(ks[18], (DEPTH, N_EXPERTS, D_MODEL))
    ln2_w = 1.0 + 0.02 * nrm(ks[19], (DEPTH, D_MODEL))
    ln2_b = 0.02 * nrm(ks[20], (DEPTH, D_MODEL))
    return {'x_prompt': x_prompt, 'x_sample': x_sample, 'cache_sb_k': cache_sb_k,
            'cache_sb_v': cache_sb_v, 'state_gla': state_gla, 'page_table': page_table,
            'w_in': w_in, 'b_sb_qk': b_sb_qk, 'w_gla_a2': w_gla_a2, 'b_gla_a': b_gla_a,
            'gla_norm_w': gla_norm_w, 'w_o': w_o, 'ln1_w': ln1_w, 'ln1_b': ln1_b,
            'w_router': w_router, 'b_router': b_router, 'w_up': w_up, 'b_up': b_up,
            'w_down': w_down, 'b_down': b_down, 'ln2_w': ln2_w, 'ln2_b': ln2_b}


def reference(x_prompt, x_sample, cache_sb_k, cache_sb_v, state_gla, page_table,
              w_in, b_sb_qk, w_gla_a2, b_gla_a, gla_norm_w, w_o, ln1_w, ln1_b,
              w_router, b_router, w_up, b_up, w_down, b_down, ln2_w, ln2_b):
    B = x_prompt.shape[0]
    Bd, Td = x_sample.shape[:2]
    n_pages = page_table.shape[1]
    past = n_pages * PAGE_SIZE
    q_pos_s = past + jnp.arange(Td, dtype=jnp.int32)
    k_pos_s = jnp.arange(past + Td, dtype=jnp.int32)
    s0_prompt = jnp.zeros((B, GLA_HEADS, GLA_DK, GLA_DV), jnp.float32)
    xp, xs = x_prompt, x_sample
    kp_rows, vp_rows, gp_states, ks_rows, vs_rows, gs_states = [], [], [], [], [], []
    for l in range(DEPTH):
        mixer_w = (w_in[l], b_sb_qk[l], w_gla_a2[l], b_gla_a[l], gla_norm_w[l], w_o[l])
        moe_w = (w_router[l], b_router[l], w_up[l], b_up[l], w_down[l], b_down[l])
        mix, kp, vp, gp = token_mixer(xp, *mixer_w, s0_prompt, sb_prompt)
        xp = layer_norm(DEEPNORM_ALPHA * xp + mix, ln1_w[l], ln1_b[l])
        xp = layer_norm(DEEPNORM_ALPHA * xp + moe(xp, *moe_w), ln2_w[l], ln2_b[l])
        kp_rows.append(kp); vp_rows.append(vp); gp_states.append(gp)
        k_past = cache_sb_k[l][page_table].reshape(Bd, past, SB_HEADS, SB_HD)
        v_past = cache_sb_v[l][page_table].reshape(Bd, past, SB_HEADS, SB_HD)
        attend = functools.partial(sb_sample, k_past=k_past, v_past=v_past, q_pos=q_pos_s, k_pos=k_pos_s)
        mix, ksm, vsm, gsm = token_mixer(xs, *mixer_w, state_gla[l], attend)
        xs = layer_norm(DEEPNORM_ALPHA * xs + mix, ln1_w[l], ln1_b[l])
        xs = layer_norm(DEEPNORM_ALPHA * xs + moe(xs, *moe_w), ln2_w[l], ln2_b[l])
        ks_rows.append(ksm); vs_rows.append(vsm); gs_states.append(gsm)
    return (xp, xs, jnp.stack(kp_rows), jnp.stack(vp_rows), jnp.stack(gp_states),
            jnp.stack(ks_rows), jnp.stack(vs_rows), jnp.stack(gs_states))
```

```python
import functools

import jax
import jax.numpy as jnp
from jax import lax
from jax.experimental import pallas as pl
from jax.experimental.pallas import tpu as pltpu

F32 = jnp.float32
BF16 = jnp.bfloat16

D_MODEL = 2048
DEPTH = 4
B_P, T_P = 2, 4096
B_S, T_S = 8, 8
N_PROMPT = B_P * T_P
N_SAMPLE = B_S * T_S
N_TOK = N_PROMPT + N_SAMPLE
N_PAD = 8448
PAGE = 128

GLA_H, GLA_DK, GLA_DV = 4, 128, 256
GLA_QK = GLA_H * GLA_DK
GLA_W = GLA_H * GLA_DV
GLA_RANK = 16
GLA_TAU = 16.0
GLA_C = 64

SB_H, SB_HD = 8, 128
SB_W = SB_H * SB_HD

N_EXP = 32
TOP_K = 4
D_EXP = D_MODEL
SWIGLU_LIMIT = 7.0
SWIGLU_ALPHA = 1.702
LN_EPS = 1e-5
DN_ALPHA = (2 * DEPTH) ** 0.25

C_QS, C_KS, C_VS, C_QKG, C_VG, C_RG, C_A = 0, 1024, 2048, 3072, 4096, 5120, 6144
H_COLS = 6272
A_PAD = 128

VMEM_LIMIT = 56 * 1024 * 1024


def _cparams(sem):
    return pltpu.CompilerParams(dimension_semantics=sem, vmem_limit_bytes=VMEM_LIMIT)


def _dot(a, b):
    return jnp.dot(a, b, preferred_element_type=F32)


def _dot_nt(a, b):
    return lax.dot_general(a, b, (((1,), (1,)), ((), ())), preferred_element_type=F32)


def _dot_tn(a, b):
    return lax.dot_general(a, b, (((0,), (0,)), ((), ())), preferred_element_type=F32)


def _split_bf16(x):
    hi = x.astype(BF16)
    lo = (x - hi.astype(F32)).astype(BF16)
    return hi, lo


def _softplus(z):
    return jnp.maximum(z, 0.0) + jnp.log(1.0 + jnp.exp(-jnp.abs(z)))


def _log_sigmoid(x):
    return jnp.minimum(x, 0.0) - jnp.log(1.0 + jnp.exp(-jnp.abs(x)))


def _sigmoid(x):
    return 1.0 / (1.0 + jnp.exp(-x))


def _iota(shape, dim):
    return lax.broadcasted_iota(jnp.int32, shape, dim)


IN_TM, IN_TN = 768, 896


def _inproj_kernel(x_ref, w_ref, b_ref, o_ref, xb_ref):
    @pl.when(pl.program_id(1) == 0)
    def _():
        xb_ref[...] = x_ref[...].astype(BF16)

    o_ref[...] = _dot(xb_ref[...], w_ref[...]) + b_ref[...]


def _inproj(x, w, b):
    return pl.pallas_call(
        _inproj_kernel,
        grid=(N_PAD // IN_TM, H_COLS // IN_TN),
        in_specs=[
            pl.BlockSpec((IN_TM, D_MODEL), lambda i, j: (i, 0)),
            pl.BlockSpec((D_MODEL, IN_TN), lambda i, j: (0, j)),
            pl.BlockSpec((1, IN_TN), lambda i, j: (0, j)),
        ],
        out_specs=pl.BlockSpec((IN_TM, IN_TN), lambda i, j: (i, j)),
        out_shape=jax.ShapeDtypeStruct((N_PAD, H_COLS), F32),
        scratch_shapes=[pltpu.VMEM((IN_TM, D_MODEL), BF16)],
        compiler_params=_cparams(("parallel", "arbitrary")),
        name="inproj",
    )(x, w, b)


def _gla_chunk(qk, vv, rr, a, wa, ba, nw, s_list, valid):
    c = GLA_C
    half = c // 2
    pre = _dot(a.astype(BF16), wa) + ba
    la = _log_sigmoid(pre) * (1.0 / GLA_TAU)
    if valid is not None:
        la = jnp.where(valid, la, 0.0)
    la_hi, la_lo = _split_bf16(la)
    tril = _iota((c, c), 0) >= _iota((c, c), 1)
    ltri = jnp.where(tril, 1.0, 0.0).astype(BF16)
    bcum = _dot(ltri, la_hi) + _dot(ltri, la_lo)
    tril_h = _iota((half, half), 0) >= _iota((half, half), 1)

    def diag_scores(qx, kx, bx):
        m = bx[half // 2:half // 2 + 1, :]
        s = _dot_nt((qx * jnp.exp(bx - m)).astype(BF16), (kx * jnp.exp(m - bx)).astype(BF16))
        return jnp.where(tril_h, s, 0.0)

    outs, s_new = [], []
    for h in range(GLA_H):
        q = qk[:, h * GLA_DK:(h + 1) * GLA_DK] * (GLA_DK ** -0.5)
        k = qk[:, GLA_QK + h * GLA_DK:GLA_QK + (h + 1) * GLA_DK]
        if valid is not None:
            k = jnp.where(valid, k, 0.0)
        bh = bcum[:, h * GLA_DK:(h + 1) * GLA_DK]
        vb = vv[:, h * GLA_DV:(h + 1) * GLA_DV].astype(BF16)
        s_old = s_list[h]
        o = _dot((q * jnp.exp(bh)).astype(BF16), s_old.astype(BF16))
        b_last = bh[c - 1:c, :]
        k_state = (k * jnp.exp(b_last - bh)).astype(BF16)
        r = bh[half - 1:half, :]
        q2 = (q[half:] * jnp.exp(bh[half:] - r)).astype(BF16)
        k1 = (k[:half] * jnp.exp(r - bh[:half])).astype(BF16)
        s21 = _dot_nt(q2, k1)
        s11 = diag_scores(q[:half], k[:half], bh[:half])
        s22 = diag_scores(q[half:], k[half:], bh[half:])
        o1 = _dot(s11.astype(BF16), vb[:half])
        o2 = _dot(s21.astype(BF16), vb[:half]) + _dot(s22.astype(BF16), vb[half:])
        o = o + jnp.concatenate([o1, o2], axis=0)
        dec = jnp.transpose(jnp.broadcast_to(jnp.exp(b_last), (GLA_DK, GLA_DK)))
        s_new.append(jnp.concatenate([dec, dec], axis=1) * s_old + _dot_tn(k_state, vb))
        mu = jnp.mean(o, axis=-1, keepdims=True)
        d = o - mu
        var = jnp.mean(d * d, axis=-1, keepdims=True)
        on = d * lax.rsqrt(var + LN_EPS) * nw[:, h * GLA_DV:(h + 1) * GLA_DV]
        rg = rr[:, h * GLA_DV:(h + 1) * GLA_DV]
        outs.append(on * (rg * _sigmoid(rg)))
    return jnp.concatenate(outs, axis=1), s_new


def _gla_prompt_kernel(qk_ref, v_ref, r_ref, a_ref, wa_ref, ba_ref, nw_ref, o_ref, sfin_ref, s_scr):
    j = pl.program_id(1)

    @pl.when(j == 0)
    def _():
        s_scr[...] = jnp.zeros_like(s_scr)

    out, s_new = _gla_chunk(qk_ref[...], v_ref[...], r_ref[...], a_ref[...], wa_ref[...], ba_ref[...],
                            nw_ref[...], [s_scr[h] for h in range(GLA_H)], None)
    o_ref[...] = out
    for h in range(GLA_H):
        s_scr[h] = s_new[h]

    @pl.when(j == pl.num_programs(1) - 1)
    def _():
        sfin_ref[0] = s_scr[...]


def _gla_prompt(hmat, wa, ba, nw):
    nchunk = T_P // GLA_C
    row = lambda b, j: b * nchunk + j
    return pl.pallas_call(
        _gla_prompt_kernel,
        grid=(B_P, nchunk),
        in_specs=[
            pl.BlockSpec((GLA_C, 1024), lambda b, j: (row(b, j), C_QKG // 1024)),
            pl.BlockSpec((GLA_C, 1024), lambda b, j: (row(b, j), C_VG // 1024)),
            pl.BlockSpec((GLA_C, 1024), lambda b, j: (row(b, j), C_RG // 1024)),
            pl.BlockSpec((GLA_C, A_PAD), lambda b, j: (row(b, j), C_A // A_PAD)),
            pl.BlockSpec((A_PAD, GLA_QK), lambda b, j: (0, 0)),
            pl.BlockSpec((1, GLA_QK), lambda b, j: (0, 0)),
            pl.BlockSpec((1, GLA_W), lambda b, j: (0, 0)),
        ],
        out_specs=[
            pl.BlockSpec((GLA_C, GLA_W), lambda b, j: (row(b, j), 0)),
            pl.BlockSpec((1, GLA_H, GLA_DK, GLA_DV), lambda b, j: (b, 0, 0, 0)),
        ],
        out_shape=[
            jax.ShapeDtypeStruct((N_PAD, GLA_W), F32),
            jax.ShapeDtypeStruct((B_P, GLA_H, GLA_DK, GLA_DV), F32),
        ],
        scratch_shapes=[pltpu.VMEM((GLA_H, GLA_DK, GLA_DV), F32)],
        compiler_params=_cparams(("parallel", "arbitrary")),
        name="gla_prompt",
    )(hmat, hmat, hmat, hmat, wa, ba, nw)


TAIL = N_PAD - N_PROMPT


def _pad_rows(x, rows):
    return jnp.concatenate([x, jnp.zeros((rows - x.shape[0], x.shape[1]), x.dtype)], axis=0)


def _gla_sample_kernel(qk_ref, v_ref, r_ref, a_ref, wa_ref, ba_ref, nw_ref, s0_ref, oin_ref, o_ref, sfin_ref):
    del oin_ref
    b = pl.program_id(0)

    @pl.when(b == 0)
    def _():
        o_ref[...] = jnp.zeros_like(o_ref)

    valid = _iota((GLA_C, 1), 0) < T_S
    out, s_new = _gla_chunk(_pad_rows(qk_ref[...], GLA_C), _pad_rows(v_ref[...], GLA_C),
                            _pad_rows(r_ref[...], GLA_C), _pad_rows(a_ref[...], GLA_C),
                            wa_ref[...], ba_ref[...], nw_ref[...],
                            [s0_ref[0, h] for h in range(GLA_H)], valid)
    o_ref[pl.ds(pl.multiple_of(b * T_S, T_S), T_S), :] = out[:T_S]
    for h in range(GLA_H):
        sfin_ref[0, h] = s_new[h]


def _gla_sample(hmat, wa, ba, nw, s0, o_g):
    row = lambda b: N_PROMPT // T_S + b
    return pl.pallas_call(
        _gla_sample_kernel,
        grid=(B_S,),
        in_specs=[
            pl.BlockSpec((T_S, 1024), lambda b: (row(b), C_QKG // 1024)),
            pl.BlockSpec((T_S, 1024), lambda b: (row(b), C_VG // 1024)),
            pl.BlockSpec((T_S, 1024), lambda b: (row(b), C_RG // 1024)),
            pl.BlockSpec((T_S, A_PAD), lambda b: (row(b), C_A // A_PAD)),
            pl.BlockSpec((A_PAD, GLA_QK), lambda b: (0, 0)),
            pl.BlockSpec((1, GLA_QK), lambda b: (0, 0)),
            pl.BlockSpec((1, GLA_W), lambda b: (0, 0)),
            pl.BlockSpec((1, GLA_H, GLA_DK, GLA_DV), lambda b: (b, 0, 0, 0)),
            pl.BlockSpec(memory_space=pl.ANY),
        ],
        out_specs=[
            pl.BlockSpec((TAIL, GLA_W), lambda b: (N_PROMPT // TAIL, 0)),
            pl.BlockSpec((1, GLA_H, GLA_DK, GLA_DV), lambda b: (b, 0, 0, 0)),
        ],
        out_shape=[
            jax.ShapeDtypeStruct((N_PAD, GLA_W), F32),
            jax.ShapeDtypeStruct((B_S, GLA_H, GLA_DK, GLA_DV), F32),
        ],
        input_output_aliases={8: 0},
        compiler_params=_cparams(("arbitrary",)),
        name="gla_sample",
    )(hmat, hmat, hmat, hmat, wa, ba, nw, s0, o_g)


SBP_TQ, SBP_TK = 256, 128


def _sb_block(q, ks, vs, carry, acc, utri, live):
    z = _dot_nt(q, ks)
    sp = _softplus(z)
    if live is not None:
        sp = jnp.where(live, sp, 0.0)
    hi, lo = _split_bf16(sp)
    rin = _dot(hi, utri) + _dot(lo, utri)
    w = jnp.exp(z - (rin + carry))
    if live is not None:
        w = jnp.where(live, w, 0.0)
    return carry + rin[:, 0:1], acc + _dot(w.astype(BF16), vs)


def _suffix_ones(n):
    return jnp.where(_iota((n, n), 0) >= _iota((n, n), 1), 1.0, 0.0).astype(BF16)


def _sbp_kernel(q_ref, k_ref, v_ref, o_ref, kb_scr, vb_scr):
    qi = pl.program_id(2)

    @pl.when(qi == 0)
    def _():
        kb_scr[...] = k_ref[...].astype(BF16)
        vb_scr[...] = v_ref[...].astype(BF16)

    tq, tk = SBP_TQ, SBP_TK
    ndiag = tq // tk
    q = (q_ref[...] * (SB_HD ** -0.5)).astype(BF16)
    utri = _suffix_ones(tk)
    qpos = qi * tq + _iota((tq, tk), 0)
    carry = jnp.zeros((tq, 1), F32)
    acc = jnp.zeros((tq, SB_HD), F32)
    for d in range(ndiag - 1, -1, -1):
        kb = qi * ndiag + d
        sl = pl.ds(pl.multiple_of(kb * tk, tk), tk)
        live = (kb * tk + _iota((tq, tk), 1)) < qpos
        carry, acc = _sb_block(q, kb_scr[sl, :], vb_scr[sl, :], carry, acc, utri, live)

    def body(it, ca):
        kb = qi * ndiag - 1 - it
        sl = pl.ds(pl.multiple_of(kb * tk, tk), tk)
        return _sb_block(q, kb_scr[sl, :], vb_scr[sl, :], ca[0], ca[1], utri, None)

    carry, acc = lax.fori_loop(0, qi * ndiag, body, (carry, acc))
    o_ref[...] = acc


def _sb_prompt(hmat):
    nq = T_P // SBP_TQ
    return pl.pallas_call(
        _sbp_kernel,
        grid=(B_P, SB_H, nq),
        in_specs=[
            pl.BlockSpec((SBP_TQ, SB_HD), lambda b, h, i: (b * nq + i, C_QS // SB_HD + h)),
            pl.BlockSpec((T_P, SB_HD), lambda b, h, i: (b, C_KS // SB_HD + h)),
            pl.BlockSpec((T_P, SB_HD), lambda b, h, i: (b, C_VS // SB_HD + h)),
        ],
        out_specs=pl.BlockSpec((SBP_TQ, SB_HD), lambda b, h, i: (b * nq + i, h)),
        out_shape=jax.ShapeDtypeStruct((N_PAD, SB_W), F32),
        scratch_shapes=[pltpu.VMEM((T_P, SB_HD), BF16), pltpu.VMEM((T_P, SB_HD), BF16)],
        compiler_params=_cparams(("parallel", "parallel", "arbitrary")),
        name="sb_prompt",
    )(hmat, hmat, hmat)


SBS_PP = 4
SBS_ROWS = SB_H * T_S


def _sbs_kernel(pt_ref, q_ref, kn_ref, vn_ref, *rest):
    del pt_ref
    k_refs = rest[:SBS_PP]
    v_refs = rest[SBS_PP:2 * SBS_PP]
    oin_ref, o_ref, qbd_scr, carry_scr, acc_scr = rest[2 * SBS_PP:]
    del oin_ref
    b = pl.program_id(0)
    j = pl.program_id(1)
    utri = _suffix_ones(PAGE)

    @pl.when(jnp.logical_and(b == 0, j == 0))
    def _():
        o_ref[...] = jnp.zeros_like(o_ref)

    @pl.when(j == 0)
    def _():
        qrep = jnp.concatenate([q_ref[...]] * SB_H, axis=0) * (SB_HD ** -0.5)
        own = (_iota((SBS_ROWS, SB_W), 0) >> 3) == (_iota((SBS_ROWS, SB_W), 1) >> 7)
        qbd = jnp.where(own, qrep, 0.0).astype(BF16)
        qbd_scr[...] = qbd
        kn = _pad_rows(kn_ref[...], PAGE).astype(BF16)
        vn = _pad_rows(vn_ref[...], PAGE).astype(BF16)
        live = _iota((SBS_ROWS, PAGE), 1) < (_iota((SBS_ROWS, PAGE), 0) & (T_S - 1))
        carry, acc = _sb_block(qbd, kn, vn, jnp.zeros((SBS_ROWS, 1), F32),
                               jnp.zeros((SBS_ROWS, SB_W), F32), utri, live)
        carry_scr[...] = carry
        acc_scr[...] = acc

    qbd = qbd_scr[...]
    carry = carry_scr[...]
    acc = acc_scr[...]
    for i in range(SBS_PP):
        carry, acc = _sb_block(qbd, k_refs[i][...].astype(BF16), v_refs[i][...].astype(BF16),
                               carry, acc, utri, None)
    carry_scr[...] = carry
    acc_scr[...] = acc

    @pl.when(j == pl.num_programs(1) - 1)
    def _():
        out = jnp.concatenate(
            [acc[h * T_S:(h + 1) * T_S, h * SB_HD:(h + 1) * SB_HD] for h in range(SB_H)], axis=1)
        o_ref[pl.ds(pl.multiple_of(b * T_S, T_S), T_S), :] = out


def _sb_sample(hmat, cache_k, cache_v, page_table, layer, o_s):
    n_pages = page_table.shape[1]
    nsteps = n_pages // SBS_PP
    row = lambda b: N_PROMPT // T_S + b

    def page_map(i):
        return lambda b, j, pt: (layer, pt[b, n_pages - 1 - (j * SBS_PP + i)], 0, 0)

    page_spec = lambda i: pl.BlockSpec((None, None, PAGE, SB_W), page_map(i))
    grid_spec = pltpu.PrefetchScalarGridSpec(
        num_scalar_prefetch=1,
        grid=(B_S, nsteps),
        in_specs=[
            pl.BlockSpec((T_S, SB_W), lambda b, j, pt: (row(b), C_QS // SB_W)),
            pl.BlockSpec((T_S, SB_W), lambda b, j, pt: (row(b), C_KS // SB_W)),
            pl.BlockSpec((T_S, SB_W), lambda b, j, pt: (row(b), C_VS // SB_W)),
        ] + [page_spec(i) for i in range(SBS_PP)] + [page_spec(i) for i in range(SBS_PP)] + [
            pl.BlockSpec(memory_space=pl.ANY),
        ],
        out_specs=pl.BlockSpec((TAIL, SB_W), lambda b, j, pt: (N_PROMPT // TAIL, 0)),
        scratch_shapes=[
            pltpu.VMEM((SBS_ROWS, SB_W), BF16),
            pltpu.VMEM((SBS_ROWS, 1), F32),
            pltpu.VMEM((SBS_ROWS, SB_W), F32),
        ],
    )
    n_in = 1 + 3 + 2 * SBS_PP
    return pl.pallas_call(
        _sbs_kernel,
        grid_spec=grid_spec,
        out_shape=jax.ShapeDtypeStruct((N_PAD, SB_W), F32),
        input_output_aliases={n_in: 0},
        compiler_params=_cparams(("arbitrary", "arbitrary")),
        name="sb_sample",
    )(page_table, hmat, hmat, hmat, *([cache_k] * SBS_PP), *([cache_v] * SBS_PP), o_s)


OP_TM = 256
R_PAD = 128


def _layer_norm(y, w, b):
    mu = jnp.mean(y, axis=-1, keepdims=True)
    d = y - mu
    var = jnp.mean(d * d, axis=-1, keepdims=True)
    return d * lax.rsqrt(var + LN_EPS) * w + b


def _outproj_kernel(og_ref, os_ref, x_ref, wo_ref, lw_ref, lb_ref, wr_ref, br_ref, x1_ref, xb_ref, lg_ref):
    mix = _dot(og_ref[...].astype(BF16), wo_ref[:GLA_W, :]) + _dot(os_ref[...].astype(BF16), wo_ref[GLA_W:, :])
    x1 = _layer_norm(DN_ALPHA * x_ref[...] + mix, lw_ref[...], lb_ref[...])
    x1_ref[...] = x1
    xb_ref[...] = x1.astype(BF16)
    hi, lo = _split_bf16(x1)
    p = _dot(hi, wr_ref[...])
    q = _dot(lo, wr_ref[:, :R_PAD])
    lg_ref[...] = p[:, :R_PAD] + p[:, R_PAD:] + q + br_ref[...]


def _outproj(o_g, o_s, x, wo, lw, lb, wr, br):
    row = lambda i: (i, 0)
    fix = lambda i: (0, 0)
    return pl.pallas_call(
        _outproj_kernel,
        grid=(N_PAD // OP_TM,),
        in_specs=[
            pl.BlockSpec((OP_TM, GLA_W), row),
            pl.BlockSpec((OP_TM, SB_W), row),
            pl.BlockSpec((OP_TM, D_MODEL), row),
            pl.BlockSpec((D_MODEL, D_MODEL), fix),
            pl.BlockSpec((1, D_MODEL), fix),
            pl.BlockSpec((1, D_MODEL), fix),
            pl.BlockSpec((D_MODEL, 2 * R_PAD), fix),
            pl.BlockSpec((1, R_PAD), fix),
        ],
        out_specs=[
            pl.BlockSpec((OP_TM, D_MODEL), row),
            pl.BlockSpec((OP_TM, D_MODEL), row),
            pl.BlockSpec((OP_TM, R_PAD), row),
        ],
        out_shape=[
            jax.ShapeDtypeStruct((N_PAD, D_MODEL), F32),
            jax.ShapeDtypeStruct((N_PAD, D_MODEL), BF16),
            jax.ShapeDtypeStruct((N_PAD, R_PAD), F32),
        ],
        compiler_params=_cparams(("parallel",)),
        name="outproj_ln1",
    )(o_g, o_s, x, wo, lw, lb, wr, br)


MOE_BM = 256
MOE_TF = 512
N_ASG = N_TOK * TOP_K
MOE_NBLK = N_ASG // MOE_BM + N_EXP
MOE_NSLOT = MOE_NBLK * MOE_BM
MOE_NCH = D_EXP // MOE_TF


def _moe_kernel(be_ref, nu_ref, x_ref, wg_ref, wu_ref, bg_ref, bu_ref, wd_ref, bd_ref, y_ref):
    del be_ref
    i = pl.program_id(0)
    c = pl.program_id(1)
    used = i < nu_ref[0]

    @pl.when(c == 0)
    def _():
        y_ref[...] = jnp.broadcast_to(bd_ref[...], y_ref.shape)

    @pl.when(used)
    def _():
        x = x_ref[...]
        g = _dot(x, wg_ref[...].astype(BF16)) + bg_ref[...]
        u = _dot(x, wu_ref[...].astype(BF16)) + bu_ref[...]
        g = jnp.minimum(g, SWIGLU_LIMIT)
        u = jnp.clip(u, -SWIGLU_LIMIT, SWIGLU_LIMIT)
        act = (u + 1.0) * (g * _sigmoid(SWIGLU_ALPHA * g))
        y_ref[...] += _dot(act.astype(BF16), wd_ref[...].astype(BF16))


def _moe_experts(block_e, n_used, xs, w_up, b_up, w_down, b_down, layer):
    def blk(i, nu):
        return jnp.minimum(i, nu[0] - 1)

    grid_spec = pltpu.PrefetchScalarGridSpec(
        num_scalar_prefetch=2,
        grid=(MOE_NBLK, MOE_NCH),
        in_specs=[
            pl.BlockSpec((MOE_BM, D_MODEL), lambda i, c, be, nu: (blk(i, nu), 0)),
            pl.BlockSpec((None, None, D_MODEL, MOE_TF), lambda i, c, be, nu: (layer, be[blk(i, nu)], 0, c)),
            pl.BlockSpec((None, None, D_MODEL, MOE_TF), lambda i, c, be, nu: (layer, be[blk(i, nu)], 0, MOE_NCH + c)),
            pl.BlockSpec((None, None, 1, MOE_TF), lambda i, c, be, nu: (layer, be[blk(i, nu)], 0, c)),
            pl.BlockSpec((None, None, 1, MOE_TF), lambda i, c, be, nu: (layer, be[blk(i, nu)], 0, MOE_NCH + c)),
            pl.BlockSpec((None, None, MOE_TF, D_MODEL), lambda i, c, be, nu: (layer, be[blk(i, nu)], c, 0)),
            pl.BlockSpec((None, None, 1, D_MODEL), lambda i, c, be, nu: (layer, be[blk(i, nu)], 0, 0)),
        ],
        out_specs=pl.BlockSpec((MOE_BM, D_MODEL), lambda i, c, be, nu: (i, 0)),
    )
    return pl.pallas_call(
        _moe_kernel,
        grid_spec=grid_spec,
        out_shape=jax.ShapeDtypeStruct((MOE_NSLOT, D_MODEL), F32),
        compiler_params=_cparams(("arbitrary", "arbitrary")),
        name="moe_experts",
    )(block_e, n_used, xs, w_up, w_up, b_up, b_up, w_down, b_down)


def _route(logits):
    top_val, top_idx = lax.top_k(logits, TOP_K)
    gates = jax.nn.softmax(top_val, axis=-1)
    flat_e = top_idx.reshape(-1)
    order = jnp.argsort(flat_e)
    sorted_e = flat_e[order]
    counts = jnp.zeros((N_EXP,), jnp.int32).at[flat_e].add(1)
    starts = jnp.cumsum(counts) - counts
    padded = (counts + MOE_BM - 1) // MOE_BM * MOE_BM
    pad_ends = jnp.cumsum(padded)
    pad_starts = pad_ends - padded
    dest = pad_starts[sorted_e] + (jnp.arange(N_ASG, dtype=jnp.int32) - starts[sorted_e])
    slot_tok = jnp.full((MOE_NSLOT,), N_TOK, jnp.int32).at[dest].set((order // TOP_K).astype(jnp.int32))
    slot_of_asg = jnp.zeros((N_ASG,), jnp.int32).at[order].set(dest.astype(jnp.int32))
    block_e = jnp.minimum(
        jnp.searchsorted(pad_ends, jnp.arange(MOE_NBLK, dtype=jnp.int32) * MOE_BM, side="right"),
        N_EXP - 1).astype(jnp.int32)
    n_used = (pad_ends[-1:] // MOE_BM).astype(jnp.int32)
    return gates, slot_tok, slot_of_asg, block_e, n_used


LN2_TM = 256


def _ln2_kernel(x_ref, y_ref, g_ref, lw_ref, lb_ref, o_ref):
    g = g_ref[...]
    moe = y_ref[0] * g[:, 0:1]
    for k in range(1, TOP_K):
        moe = moe + y_ref[k] * g[:, k:k + 1]
    o_ref[...] = _layer_norm(DN_ALPHA * x_ref[...] + moe, lw_ref[...], lb_ref[...])


def _ln2(x1, yk, gates, lw, lb):
    return pl.pallas_call(
        _ln2_kernel,
        grid=(N_PAD // LN2_TM,),
        in_specs=[
            pl.BlockSpec((LN2_TM, D_MODEL), lambda i: (i, 0)),
            pl.BlockSpec((TOP_K, LN2_TM, D_MODEL), lambda i: (0, i, 0)),
            pl.BlockSpec((LN2_TM, TOP_K), lambda i: (i, 0)),
            pl.BlockSpec((1, D_MODEL), lambda i: (0, 0)),
            pl.BlockSpec((1, D_MODEL), lambda i: (0, 0)),
        ],
        out_specs=pl.BlockSpec((LN2_TM, D_MODEL), lambda i: (i, 0)),
        out_shape=jax.ShapeDtypeStruct((N_PAD, D_MODEL), F32),
        compiler_params=_cparams(("parallel",)),
        name="combine_ln2",
    )(x1, yk, gates, lw, lb)


def _prep_w_in(w_in, b_sb_qk):
    o_qg, o_kg, o_vg, o_rg, o_a, o_qs = 0, 512, 1024, 2048, 3072, 3088
    o_ks, o_vs = o_qs + SB_W, o_qs + 2 * SB_W
    cols = [
        w_in[:, o_qs:o_qs + SB_W], w_in[:, o_ks:o_ks + SB_W], w_in[:, o_vs:o_vs + SB_W],
        w_in[:, o_qg:o_vg], w_in[:, o_vg:o_rg], w_in[:, o_rg:o_a],
        w_in[:, o_a:o_qs], jnp.zeros((D_MODEL, A_PAD - GLA_RANK), w_in.dtype),
    ]
    w = jnp.concatenate(cols, axis=1).astype(BF16)
    bias = jnp.concatenate([b_sb_qk, jnp.zeros((H_COLS - 2 * SB_W,), F32)]).reshape(1, H_COLS)
    return w, bias


def _prep_router(w_router, b_router):
    wp = jnp.pad(w_router, ((0, 0), (0, R_PAD - N_EXP)))
    hi = wp.astype(BF16)
    lo = (wp - hi.astype(F32)).astype(BF16)
    bp = jnp.pad(b_router, (0, R_PAD - N_EXP)).reshape(1, R_PAD)
    return jnp.concatenate([hi, lo], axis=1), bp


def kernel(x_prompt, x_sample, cache_sb_k, cache_sb_v, state_gla, page_table, w_in, b_sb_qk, w_gla_a2, b_gla_a,
           gla_norm_w, w_o, ln1_w, ln1_b, w_router, b_router, w_up, b_up, w_down, b_down, ln2_w, ln2_b):
    n_pool = cache_sb_k.shape[1]
    cache_k = cache_sb_k.reshape(DEPTH, n_pool, PAGE, SB_W)
    cache_v = cache_sb_v.reshape(DEPTH, n_pool, PAGE, SB_W)
    b_up4 = b_up.reshape(DEPTH, N_EXP, 1, 2 * D_EXP)
    b_down4 = b_down.reshape(DEPTH, N_EXP, 1, D_MODEL)
    x = jnp.concatenate([x_prompt.reshape(N_PROMPT, D_MODEL), x_sample.reshape(N_SAMPLE, D_MODEL),
                         jnp.zeros((N_PAD - N_TOK, D_MODEL), F32)], axis=0)
    kp, vp, gp, ks, vs, gs = [], [], [], [], [], []
    for l in range(DEPTH):
        w1, bias1 = _prep_w_in(w_in[l], b_sb_qk[l])
        hmat = _inproj(x, w1, bias1)
        wa = jnp.pad(w_gla_a2[l], ((0, A_PAD - GLA_RANK), (0, 0))).astype(BF16)
        ba = b_gla_a[l].reshape(1, GLA_QK)
        nw = gla_norm_w[l].reshape(1, GLA_W)
        o_g, g_p = _gla_prompt(hmat, wa, ba, nw)
        o_g, g_s = _gla_sample(hmat, wa, ba, nw, state_gla[l], o_g)
        o_s = _sb_prompt(hmat)
        o_s = _sb_sample(hmat, cache_k, cache_v, page_table, l, o_s)
        wr, br = _prep_router(w_router[l], b_router[l])
        x1, x1b, logits = _outproj(o_g, o_s, x, w_o[l].astype(BF16), ln1_w[l].reshape(1, D_MODEL),
                                   ln1_b[l].reshape(1, D_MODEL), wr, br)
        gates, slot_tok, slot_of_asg, block_e, n_used = _route(logits[:N_TOK, :N_EXP])
        xs = jnp.take(x1b, slot_tok, axis=0)
        ys = _moe_experts(block_e, n_used, xs, w_up, b_up4, w_down, b_down4, l)
        slot_k = jnp.pad(slot_of_asg.reshape(N_TOK, TOP_K), ((0, N_PAD - N_TOK), (0, 0))).T
        yk = jnp.take(ys, slot_k, axis=0)
        gates_p = jnp.pad(gates, ((0, N_PAD - N_TOK), (0, 0)))
        x = _ln2(x1, yk, gates_p, ln2_w[l].reshape(1, D_MODEL), ln2_b[l].reshape(1, D_MODEL))
        kp.append(hmat[:N_PROMPT, C_KS:C_KS + SB_W].reshape(B_P, T_P, SB_H, SB_HD))
        vp.append(hmat[:N_PROMPT, C_VS:C_VS + SB_W].reshape(B_P, T_P, SB_H, SB_HD))
        ks.append(hmat[N_PROMPT:N_TOK, C_KS:C_KS + SB_W].reshape(B_S, T_S, SB_H, SB_HD))
        vs.append(hmat[N_PROMPT:N_TOK, C_VS:C_VS + SB_W].reshape(B_S, T_S, SB_H, SB_HD))
        gp.append(g_p)
        gs.append(g_s)
    return (x[:N_PROMPT].reshape(B_P, T_P, D_MODEL), x[N_PROMPT:N_TOK].reshape(B_S, T_S, D_MODEL),
            jnp.stack(kp), jnp.stack(vp), jnp.stack(gp), jnp.stack(ks), jnp.stack(vs), jnp.stack(gs))
```

```python
import functools

import jax
import jax.numpy as jnp
from jax import lax
from jax.experimental import pallas as pl
from jax.experimental.pallas import tpu as pltpu

F32 = jnp.float32
BF16 = jnp.bfloat16

D_MODEL = 2048
DEPTH = 4
B_P, T_P = 2, 4096
B_S, T_S = 8, 8
N_PROMPT = B_P * T_P
N_SAMPLE = B_S * T_S
N_TOK = N_PROMPT + N_SAMPLE
N_PAD = 8448
PAGE = 128

GLA_H, GLA_DK, GLA_DV = 4, 128, 256
GLA_QK = GLA_H * GLA_DK
GLA_W = GLA_H * GLA_DV
GLA_RANK = 16
GLA_TAU = 16.0
GLA_C = 64

SB_H, SB_HD = 8, 128
SB_W = SB_H * SB_HD

N_EXP = 32
TOP_K = 4
D_EXP = D_MODEL
SWIGLU_LIMIT = 7.0
SWIGLU_ALPHA = 1.702
LN_EPS = 1e-5
DN_ALPHA = (2 * DEPTH) ** 0.25

C_QS, C_KS, C_VS, C_QKG, C_VG, C_RG, C_A = 0, 1024, 2048, 3072, 4096, 5120, 6144
H_COLS = 6272
A_PAD = 128

VMEM_LIMIT = 56 * 1024 * 1024


def _cparams(sem):
    return pltpu.CompilerParams(dimension_semantics=sem, vmem_limit_bytes=VMEM_LIMIT)


def _dot(a, b):
    return jnp.dot(a, b, preferred_element_type=F32)


def _dot_nt(a, b):
    return lax.dot_general(a, b, (((1,), (1,)), ((), ())), preferred_element_type=F32)


def _dot_tn(a, b):
    return lax.dot_general(a, b, (((0,), (0,)), ((), ())), preferred_element_type=F32)


def _split_bf16(x):
    hi = x.astype(BF16)
    lo = (x - hi.astype(F32)).astype(BF16)
    return hi, lo


def _softplus(z):
    return jnp.maximum(z, 0.0) + jnp.log(1.0 + jnp.exp(-jnp.abs(z)))


def _log_sigmoid(x):
    return jnp.minimum(x, 0.0) - jnp.log(1.0 + jnp.exp(-jnp.abs(x)))


def _sigmoid(x):
    return 1.0 / (1.0 + jnp.exp(-x))


def _iota(shape, dim):
    return lax.broadcasted_iota(jnp.int32, shape, dim)


IN_TM, IN_TN = 768, 896


def _inproj_kernel(x_ref, w_ref, b_ref, o_ref, xb_ref):
    @pl.when(pl.program_id(1) == 0)
    def _():
        xb_ref[...] = x_ref[...].astype(BF16)

    o_ref[...] = _dot(xb_ref[...], w_ref[...]) + b_ref[...]


def _inproj(x, w, b):
    return pl.pallas_call(
        _inproj_kernel,
        grid=(N_PAD // IN_TM, H_COLS // IN_TN),
        in_specs=[
            pl.BlockSpec((IN_TM, D_MODEL), lambda i, j: (i, 0)),
            pl.BlockSpec((D_MODEL, IN_TN), lambda i, j: (0, j)),
            pl.BlockSpec((1, IN_TN), lambda i, j: (0, j)),
        ],
        out_specs=pl.BlockSpec((IN_TM, IN_TN), lambda i, j: (i, j)),
        out_shape=jax.ShapeDtypeStruct((N_PAD, H_COLS), F32),
        scratch_shapes=[pltpu.VMEM((IN_TM, D_MODEL), BF16)],
        compiler_params=_cparams(("parallel", "arbitrary")),
        name="inproj",
    )(x, w, b)


def _gla_chunk(qk, vv, rr, a, wa, ba, nw, s_list, valid):
    c = GLA_C
    half = c // 2
    pre = _dot(a.astype(BF16), wa) + ba
    la = _log_sigmoid(pre) * (1.0 / GLA_TAU)
    if valid is not None:
        la = jnp.where(valid, la, 0.0)
    la_hi, la_lo = _split_bf16(la)
    tril = _iota((c, c), 0) >= _iota((c, c), 1)
    ltri = jnp.where(tril, 1.0, 0.0).astype(BF16)
    bcum = _dot(ltri, la_hi) + _dot(ltri, la_lo)
    tril_h = _iota((half, half), 0) >= _iota((half, half), 1)

    def diag_scores(qx, kx, bx):
        m = bx[half // 2:half // 2 + 1, :]
        s = _dot_nt((qx * jnp.exp(bx - m)).astype(BF16), (kx * jnp.exp(m - bx)).astype(BF16))
        return jnp.where(tril_h, s, 0.0)

    outs, s_new = [], []
    for h in range(GLA_H):
        q = qk[:, h * GLA_DK:(h + 1) * GLA_DK] * (GLA_DK ** -0.5)
        k = qk[:, GLA_QK + h * GLA_DK:GLA_QK + (h + 1) * GLA_DK]
        if valid is not None:
            k = jnp.where(valid, k, 0.0)
        bh = bcum[:, h * GLA_DK:(h + 1) * GLA_DK]
        vb = vv[:, h * GLA_DV:(h + 1) * GLA_DV].astype(BF16)
        s_old = s_list[h]
        o = _dot((q * jnp.exp(bh)).astype(BF16), s_old.astype(BF16))
        b_last = bh[c - 1:c, :]
        k_state = (k * jnp.exp(b_last - bh)).astype(BF16)
        r = bh[half - 1:half, :]
        q2 = (q[half:] * jnp.exp(bh[half:] - r)).astype(BF16)
        k1 = (k[:half] * jnp.exp(r - bh[:half])).astype(BF16)
        s21 = _dot_nt(q2, k1)
        s11 = diag_scores(q[:half], k[:half], bh[:half])
        s22 = diag_scores(q[half:], k[half:], bh[half:])
        o1 = _dot(s11.astype(BF16), vb[:half])
        o2 = _dot(s21.astype(BF16), vb[:half]) + _dot(s22.astype(BF16), vb[half:])
        o = o + jnp.concatenate([o1, o2], axis=0)
        dec = jnp.transpose(jnp.broadcast_to(jnp.exp(b_last), (GLA_DK, GLA_DK)))
        s_new.append(jnp.concatenate([dec, dec], axis=1) * s_old + _dot_tn(k_state, vb))
        mu = jnp.mean(o, axis=-1, keepdims=True)
        d = o - mu
        var = jnp.mean(d * d, axis=-1, keepdims=True)
        on = d * lax.rsqrt(var + LN_EPS) * nw[:, h * GLA_DV:(h + 1) * GLA_DV]
        rg = rr[:, h * GLA_DV:(h + 1) * GLA_DV]
        outs.append(on * (rg * _sigmoid(rg)))
    return jnp.concatenate(outs, axis=1), s_new


def _gla_prompt_kernel(qk_ref, v_ref, r_ref, a_ref, wa_ref, ba_ref, nw_ref, o_ref, sfin_ref, s_scr):
    j = pl.program_id(1)

    @pl.when(j == 0)
    def _():
        s_scr[...] = jnp.zeros_like(s_scr)

    out, s_new = _gla_chunk(qk_ref[...], v_ref[...], r_ref[...], a_ref[...], wa_ref[...], ba_ref[...],
                            nw_ref[...], [s_scr[h] for h in range(GLA_H)], None)
    o_ref[...] = out
    for h in range(GLA_H):
        s_scr[h] = s_new[h]

    @pl.when(j == pl.num_programs(1) - 1)
    def _():
        sfin_ref[0] = s_scr[...]


def _gla_prompt(hmat, wa, ba, nw):
    nchunk = T_P // GLA_C
    row = lambda b, j: b * nchunk + j
    return pl.pallas_call(
        _gla_prompt_kernel,
        grid=(B_P, nchunk),
        in_specs=[
            pl.BlockSpec((GLA_C, 1024), lambda b, j: (row(b, j), C_QKG // 1024)),
            pl.BlockSpec((GLA_C, 1024), lambda b, j: (row(b, j), C_VG // 1024)),
            pl.BlockSpec((GLA_C, 1024), lambda b, j: (row(b, j), C_RG // 1024)),
            pl.BlockSpec((GLA_C, A_PAD), lambda b, j: (row(b, j), C_A // A_PAD)),
            pl.BlockSpec((A_PAD, GLA_QK), lambda b, j: (0, 0)),
            pl.BlockSpec((1, GLA_QK), lambda b, j: (0, 0)),
            pl.BlockSpec((1, GLA_W), lambda b, j: (0, 0)),
        ],
        out_specs=[
            pl.BlockSpec((GLA_C, GLA_W), lambda b, j: (row(b, j), 0)),
            pl.BlockSpec((1, GLA_H, GLA_DK, GLA_DV), lambda b, j: (b, 0, 0, 0)),
        ],
        out_shape=[
            jax.ShapeDtypeStruct((N_PAD, GLA_W), F32),
            jax.ShapeDtypeStruct((B_P, GLA_H, GLA_DK, GLA_DV), F32),
        ],
        scratch_shapes=[pltpu.VMEM((GLA_H, GLA_DK, GLA_DV), F32)],
        compiler_params=_cparams(("parallel", "arbitrary")),
        name="gla_prompt",
    )(hmat, hmat, hmat, hmat, wa, ba, nw)


TAIL = N_PAD - N_PROMPT


def _pad_rows(x, rows):
    return jnp.concatenate([x, jnp.zeros((rows - x.shape[0], x.shape[1]), x.dtype)], axis=0)


def _gla_sample_kernel(qk_ref, v_ref, r_ref, a_ref, wa_ref, ba_ref, nw_ref, s0_ref, oin_ref, o_ref, sfin_ref):
    del oin_ref
    b = pl.program_id(0)

    @pl.when(b == 0)
    def _():
        o_ref[...] = jnp.zeros_like(o_ref)

    valid = _iota((GLA_C, 1), 0) < T_S
    out, s_new = _gla_chunk(_pad_rows(qk_ref[...], GLA_C), _pad_rows(v_ref[...], GLA_C),
                            _pad_rows(r_ref[...], GLA_C), _pad_rows(a_ref[...], GLA_C),
                            wa_ref[...], ba_ref[...], nw_ref[...],
                            [s0_ref[0, h] for h in range(GLA_H)], valid)
    o_ref[pl.ds(pl.multiple_of(b * T_S, T_S), T_S), :] = out[:T_S]
    for h in range(GLA_H):
        sfin_ref[0, h] = s_new[h]


def _gla_sample(hmat, wa, ba, nw, s0, o_g):
    row = lambda b: N_PROMPT // T_S + b
    return pl.pallas_call(
        _gla_sample_kernel,
        grid=(B_S,),
        in_specs=[
            pl.BlockSpec((T_S, 1024), lambda b: (row(b), C_QKG // 1024)),
            pl.BlockSpec((T_S, 1024), lambda b: (row(b), C_VG // 1024)),
            pl.BlockSpec((T_S, 1024), lambda b: (row(b), C_RG // 1024)),
            pl.BlockSpec((T_S, A_PAD), lambda b: (row(b), C_A // A_PAD)),
            pl.BlockSpec((A_PAD, GLA_QK), lambda b: (0, 0)),
            pl.BlockSpec((1, GLA_QK), lambda b: (0, 0)),
            pl.BlockSpec((1, GLA_W), lambda b: (0, 0)),
            pl.BlockSpec((1, GLA_H, GLA_DK, GLA_DV), lambda b: (b, 0, 0, 0)),
            pl.BlockSpec(memory_space=pl.ANY),
        ],
        out_specs=[
            pl.BlockSpec((TAIL, GLA_W), lambda b: (N_PROMPT // TAIL, 0)),
            pl.BlockSpec((1, GLA_H, GLA_DK, GLA_DV), lambda b: (b, 0, 0, 0)),
        ],
        out_shape=[
            jax.ShapeDtypeStruct((N_PAD, GLA_W), F32),
            jax.ShapeDtypeStruct((B_S, GLA_H, GLA_DK, GLA_DV), F32),
        ],
        input_output_aliases={8: 0},
        compiler_params=_cparams(("arbitrary",)),
        name="gla_sample",
    )(hmat, hmat, hmat, hmat, wa, ba, nw, s0, o_g)


SBP_T = 256
SBP_NH = 2


def _sb_weights(z, carry, utri, live):
    sp = _softplus(z)
    if live is not None:
        sp = jnp.where(live, sp, 0.0)
    hi, lo = _split_bf16(sp)
    rin = _dot(hi, utri) + _dot(lo, utri)
    w = jnp.exp(z - (rin + carry))
    if live is not None:
        w = jnp.where(live, w, 0.0)
    return w.astype(BF16), carry + rin[:, 0:1]


def _suffix_ones(n):
    return jnp.where(_iota((n, n), 0) >= _iota((n, n), 1), 1.0, 0.0).astype(BF16)


def _sbp_kernel(q_ref, k_ref, v_ref, o_ref, kb_scr, vb_scr):
    qi = pl.program_id(2)

    @pl.when(qi == 0)
    def _():
        kb_scr[...] = k_ref[...].astype(BF16)
        vb_scr[...] = v_ref[...].astype(BF16)

    t = SBP_T
    hs = [slice(h * SB_HD, (h + 1) * SB_HD) for h in range(SBP_NH)]
    qs = [(q_ref[:, hs[h]] * (SB_HD ** -0.5)).astype(BF16) for h in range(SBP_NH)]
    utri = _suffix_ones(t)

    def sweep(kb, state, live):
        sl = pl.ds(pl.multiple_of(kb * t, t), t)
        out = []
        for h in range(SBP_NH):
            carry, acc = state[h]
            w, carry = _sb_weights(_dot_nt(qs[h], kb_scr[sl, hs[h]]), carry, utri, live)
            out.append((carry, acc + _dot(w, vb_scr[sl, hs[h]])))
        return tuple(out)

    zero = (jnp.zeros((t, 1), F32), jnp.zeros((t, SB_HD), F32))
    state = sweep(qi, (zero,) * SBP_NH, _iota((t, t), 1) < _iota((t, t), 0))
    state = lax.fori_loop(0, qi, lambda it, st: sweep(qi - 1 - it, st, None), state)
    for h in range(SBP_NH):
        o_ref[:, hs[h]] = state[h][1]


def _sb_prompt(hmat):
    nq = T_P // SBP_T
    wid = SBP_NH * SB_HD
    return pl.pallas_call(
        _sbp_kernel,
        grid=(B_P, SB_H // SBP_NH, nq),
        in_specs=[
            pl.BlockSpec((SBP_T, wid), lambda b, h, i: (b * nq + i, C_QS // wid + h)),
            pl.BlockSpec((T_P, wid), lambda b, h, i: (b, C_KS // wid + h)),
            pl.BlockSpec((T_P, wid), lambda b, h, i: (b, C_VS // wid + h)),
        ],
        out_specs=pl.BlockSpec((SBP_T, wid), lambda b, h, i: (b * nq + i, h)),
        out_shape=jax.ShapeDtypeStruct((N_PAD, SB_W), F32),
        scratch_shapes=[pltpu.VMEM((T_P, wid), BF16), pltpu.VMEM((T_P, wid), BF16)],
        compiler_params=_cparams(("parallel", "parallel", "arbitrary")),
        name="sb_prompt",
    )(hmat, hmat, hmat)


SBS_PP = 8
SBS_TP = 16
SBS_ROWS = SB_H * SBS_TP


def _sbs_pages(qs, k_head, v_head, npages, carry, acc, utri, live):
    z = jnp.concatenate([_dot_nt(qs[h], k_head(h).astype(BF16)) for h in range(SB_H)], axis=0)
    sp = _softplus(z)
    if live is not None:
        sp = jnp.where(live, sp, 0.0)
    hi, lo = _split_bf16(sp)
    ws = []
    for i in range(npages):
        cols = slice(i * PAGE, (i + 1) * PAGE)
        rin = _dot(hi[:, cols], utri) + _dot(lo[:, cols], utri)
        ws.append(jnp.exp(z[:, cols] - (rin + carry)))
        carry = carry + rin[:, 0:1]
    w = jnp.concatenate(ws, axis=1) if npages > 1 else ws[0]
    if live is not None:
        w = jnp.where(live, w, 0.0)
    w = w.astype(BF16)
    pv = [_dot(w[h * SBS_TP:(h + 1) * SBS_TP], v_head(h).astype(BF16)) for h in range(SB_H)]
    return carry, acc + jnp.concatenate(pv, axis=0)


def _sbs_kernel(pt_ref, q_ref, kn_ref, vn_ref, *rest):
    del pt_ref
    k_refs = rest[:SBS_PP]
    v_refs = rest[SBS_PP:2 * SBS_PP]
    oin_ref, o_ref, q_scr, carry_scr, acc_scr = rest[2 * SBS_PP:]
    del oin_ref
    b = pl.program_id(0)
    j = pl.program_id(1)
    utri = _suffix_ones(PAGE)
    head = lambda h: slice(h * SB_HD, (h + 1) * SB_HD)

    @pl.when(jnp.logical_and(b == 0, j == 0))
    def _():
        o_ref[...] = jnp.zeros_like(o_ref)

    @pl.when(j == 0)
    def _():
        q = _pad_rows(q_ref[...] * (SB_HD ** -0.5), SBS_TP)
        q_scr[...] = jnp.concatenate([q[:, head(h)] for h in range(SB_H)], axis=0).astype(BF16)
        qs = [q[:, head(h)].astype(BF16) for h in range(SB_H)]
        kn = _pad_rows(kn_ref[...], PAGE)
        vn = _pad_rows(vn_ref[...], PAGE)
        live = _iota((SBS_ROWS, PAGE), 1) < (_iota((SBS_ROWS, PAGE), 0) & (SBS_TP - 1))
        carry, acc = _sbs_pages(qs, lambda h: kn[:, head(h)], lambda h: vn[:, head(h)], 1,
                                jnp.zeros((SBS_ROWS, 1), F32), jnp.zeros((SBS_ROWS, SB_HD), F32), utri, live)
        carry_scr[...] = carry
        acc_scr[...] = acc

    qs = [q_scr[h * SBS_TP:(h + 1) * SBS_TP, :] for h in range(SB_H)]

    def head_rows(refs):
        return lambda h: jnp.concatenate([r[pl.ds(h, PAGE, stride=SB_H), :] for r in refs], axis=0)

    carry, acc = _sbs_pages(qs, head_rows(k_refs), head_rows(v_refs), SBS_PP,
                            carry_scr[...], acc_scr[...], utri, None)
    carry_scr[...] = carry
    acc_scr[...] = acc

    @pl.when(j == pl.num_programs(1) - 1)
    def _():
        out = jnp.concatenate([acc[h * SBS_TP:h * SBS_TP + T_S, :] for h in range(SB_H)], axis=1)
        o_ref[pl.ds(pl.multiple_of(b * T_S, T_S), T_S), :] = out


def _sb_sample(hmat, cache_k, cache_v, page_table, layer, o_s):
    n_pages = page_table.shape[1]
    nsteps = n_pages // SBS_PP
    row = lambda b: N_PROMPT // T_S + b

    def page_map(i):
        return lambda b, j, pt: (layer, pt[b, n_pages - 1 - (j * SBS_PP + i)], 0, 0)

    page_spec = lambda i: pl.BlockSpec((None, None, PAGE * SB_H, SB_HD), page_map(i))
    grid_spec = pltpu.PrefetchScalarGridSpec(
        num_scalar_prefetch=1,
        grid=(B_S, nsteps),
        in_specs=[
            pl.BlockSpec((T_S, SB_W), lambda b, j, pt: (row(b), C_QS // SB_W)),
            pl.BlockSpec((T_S, SB_W), lambda b, j, pt: (row(b), C_KS // SB_W)),
            pl.BlockSpec((T_S, SB_W), lambda b, j, pt: (row(b), C_VS // SB_W)),
        ] + [page_spec(i) for i in range(SBS_PP)] + [page_spec(i) for i in range(SBS_PP)] + [
            pl.BlockSpec(memory_space=pl.ANY),
        ],
        out_specs=pl.BlockSpec((TAIL, SB_W), lambda b, j, pt: (N_PROMPT // TAIL, 0)),
        scratch_shapes=[
            pltpu.VMEM((SBS_ROWS, SB_HD), BF16),
            pltpu.VMEM((SBS_ROWS, 1), F32),
            pltpu.VMEM((SBS_ROWS, SB_HD), F32),
        ],
    )
    n_in = 1 + 3 + 2 * SBS_PP
    return pl.pallas_call(
        _sbs_kernel,
        grid_spec=grid_spec,
        out_shape=jax.ShapeDtypeStruct((N_PAD, SB_W), F32),
        input_output_aliases={n_in: 0},
        compiler_params=_cparams(("arbitrary", "arbitrary")),
        name="sb_sample",
    )(page_table, hmat, hmat, hmat, *([cache_k] * SBS_PP), *([cache_v] * SBS_PP), o_s)


OP_TM = 256
R_PAD = 128


def _layer_norm(y, w, b):
    mu = jnp.mean(y, axis=-1, keepdims=True)
    d = y - mu
    var = jnp.mean(d * d, axis=-1, keepdims=True)
    return d * lax.rsqrt(var + LN_EPS) * w + b


def _outproj_kernel(og_ref, os_ref, x_ref, wo_ref, lw_ref, lb_ref, wr_ref, br_ref, x1_ref, xb_ref, lg_ref):
    mix = _dot(og_ref[...].astype(BF16), wo_ref[:GLA_W, :]) + _dot(os_ref[...].astype(BF16), wo_ref[GLA_W:, :])
    x1 = _layer_norm(DN_ALPHA * x_ref[...] + mix, lw_ref[...], lb_ref[...])
    x1_ref[...] = x1
    xb_ref[...] = x1.astype(BF16)
    hi, lo = _split_bf16(x1)
    p = _dot(hi, wr_ref[...])
    q = _dot(lo, wr_ref[:, :R_PAD])
    lg_ref[...] = p[:, :R_PAD] + p[:, R_PAD:] + q + br_ref[...]


def _outproj(o_g, o_s, x, wo, lw, lb, wr, br):
    row = lambda i: (i, 0)
    fix = lambda i: (0, 0)
    return pl.pallas_call(
        _outproj_kernel,
        grid=(N_PAD // OP_TM,),
        in_specs=[
            pl.BlockSpec((OP_TM, GLA_W), row),
            pl.BlockSpec((OP_TM, SB_W), row),
            pl.BlockSpec((OP_TM, D_MODEL), row),
            pl.BlockSpec((D_MODEL, D_MODEL), fix),
            pl.BlockSpec((1, D_MODEL), fix),
            pl.BlockSpec((1, D_MODEL), fix),
            pl.BlockSpec((D_MODEL, 2 * R_PAD), fix),
            pl.BlockSpec((1, R_PAD), fix),
        ],
        out_specs=[
            pl.BlockSpec((OP_TM, D_MODEL), row),
            pl.BlockSpec((OP_TM, D_MODEL), row),
            pl.BlockSpec((OP_TM, R_PAD), row),
        ],
        out_shape=[
            jax.ShapeDtypeStruct((N_PAD, D_MODEL), F32),
            jax.ShapeDtypeStruct((N_PAD, D_MODEL), BF16),
            jax.ShapeDtypeStruct((N_PAD, R_PAD), F32),
        ],
        compiler_params=_cparams(("parallel",)),
        name="outproj_ln1",
    )(o_g, o_s, x, wo, lw, lb, wr, br)


MOE_BM = 256
MOE_CAP = 5
MOE_GR = MOE_BM * MOE_CAP
MOE_TF = 512
N_ASG = N_TOK * TOP_K
MOE_NG = N_ASG // MOE_GR + N_EXP
MOE_NSLOT = MOE_NG * MOE_GR
MOE_NCH = D_EXP // MOE_TF


def _moe_kernel(ge_ref, gn_ref, gx_ref, x_ref, wg_ref, wu_ref, bg_ref, bu_ref, wd_ref, bd_ref, y_ref,
                wgb, wub, wdb, acc):
    del ge_ref, gx_ref
    c = pl.program_id(1)
    n = gn_ref[pl.program_id(0)]
    last = c == pl.num_programs(1) - 1

    @pl.when(n > 0)
    def _():
        wgb[...] = wg_ref[...].astype(BF16)
        wub[...] = wu_ref[...].astype(BF16)
        wdb[...] = wd_ref[...].astype(BF16)

    for i in range(MOE_CAP):
        rows = slice(i * MOE_BM, (i + 1) * MOE_BM)

        @pl.when(jnp.logical_and(i < n, c == 0))
        def _():
            acc[rows, :] = jnp.broadcast_to(bd_ref[...], (MOE_BM, D_MODEL))

        @pl.when(i < n)
        def _():
            x = x_ref[rows, :]
            g = jnp.minimum(_dot(x, wgb[...]) + bg_ref[...], SWIGLU_LIMIT)
            u = jnp.clip(_dot(x, wub[...]) + bu_ref[...], -SWIGLU_LIMIT, SWIGLU_LIMIT)
            act = (u + 1.0) * (g * _sigmoid(SWIGLU_ALPHA * g))
            acc[rows, :] += _dot(act.astype(BF16), wdb[...])

        @pl.when(jnp.logical_and(i < n, last))
        def _():
            y_ref[rows, :] = acc[rows, :].astype(BF16)

        @pl.when(jnp.logical_and(jnp.logical_and(i >= n, n > 0), last))
        def _():
            y_ref[rows, :] = jnp.zeros((MOE_BM, D_MODEL), BF16)


def _moe_experts(g_e, g_n, g_x, xs, w_up, b_up, w_down, b_down, layer):
    chunk = lambda s, c, gn: jnp.where(gn[s] > 0, c, MOE_NCH - 1)
    grid_spec = pltpu.PrefetchScalarGridSpec(
        num_scalar_prefetch=3,
        grid=(MOE_NG, MOE_NCH),
        in_specs=[
            pl.BlockSpec((MOE_GR, D_MODEL), lambda s, c, ge, gn, gx: (gx[s], 0), pipeline_mode=pl.Buffered(1)),
            pl.BlockSpec((None, None, D_MODEL, MOE_TF), lambda s, c, ge, gn, gx: (layer, ge[s], 0, chunk(s, c, gn))),
            pl.BlockSpec((None, None, D_MODEL, MOE_TF),
                         lambda s, c, ge, gn, gx: (layer, ge[s], 0, MOE_NCH + chunk(s, c, gn))),
            pl.BlockSpec((None, None, 1, MOE_TF), lambda s, c, ge, gn, gx: (layer, ge[s], 0, chunk(s, c, gn))),
            pl.BlockSpec((None, None, 1, MOE_TF),
                         lambda s, c, ge, gn, gx: (layer, ge[s], 0, MOE_NCH + chunk(s, c, gn))),
            pl.BlockSpec((None, None, MOE_TF, D_MODEL), lambda s, c, ge, gn, gx: (layer, ge[s], chunk(s, c, gn), 0)),
            pl.BlockSpec((None, None, 1, D_MODEL), lambda s, c, ge, gn, gx: (layer, ge[s], 0, 0)),
        ],
        out_specs=pl.BlockSpec((MOE_GR, D_MODEL), lambda s, c, ge, gn, gx: (gx[s], 0), pipeline_mode=pl.Buffered(1)),
        scratch_shapes=[
            pltpu.VMEM((D_MODEL, MOE_TF), BF16),
            pltpu.VMEM((D_MODEL, MOE_TF), BF16),
            pltpu.VMEM((MOE_TF, D_MODEL), BF16),
            pltpu.VMEM((MOE_GR, D_MODEL), F32),
        ],
    )
    return pl.pallas_call(
        _moe_kernel,
        grid_spec=grid_spec,
        out_shape=jax.ShapeDtypeStruct((MOE_NSLOT, D_MODEL), BF16),
        compiler_params=_cparams(("arbitrary", "arbitrary")),
        name="moe_experts",
    )(g_e, g_n, g_x, xs, w_up, w_up, b_up, b_up, w_down, b_down)


def _route(logits):
    top_val, top_idx = lax.top_k(logits, TOP_K)
    gates = jax.nn.softmax(top_val, axis=-1)
    flat_e = top_idx.reshape(-1).astype(jnp.int32)
    asg = jnp.arange(N_ASG, dtype=jnp.int32)
    _, order = lax.sort((flat_e, asg), num_keys=1)
    _, rank = lax.sort((order, asg), num_keys=1)
    experts = jnp.arange(N_EXP, dtype=jnp.int32)
    counts = jnp.sum((flat_e[:, None] == experts[None, :]).astype(jnp.int32), axis=0)
    starts = jnp.cumsum(counts) - counts
    ng = (counts + MOE_GR - 1) // MOE_GR
    g_end = jnp.cumsum(ng)
    g_first = g_end - ng
    row0 = g_first * MOE_GR
    n_used = g_end[-1]
    s = jnp.arange(MOE_NG, dtype=jnp.int32)
    s_c = jnp.minimum(s, n_used - 1)
    g_e = jnp.sum((g_end[None, :] <= s_c[:, None]).astype(jnp.int32), axis=1)
    left = counts[g_e] - (s_c - g_first[g_e]) * MOE_GR
    g_n = jnp.where(s < n_used, jnp.clip((left + MOE_BM - 1) // MOE_BM, 0, MOE_CAP), 0).astype(jnp.int32)
    slot_of_asg = row0[flat_e] + rank - starts[flat_e]
    slot = jnp.arange(MOE_NSLOT, dtype=jnp.int32)
    e_s = jnp.repeat(g_e, MOE_GR)
    r = slot - row0[e_s]
    src = jnp.clip(starts[e_s] + r, 0, N_ASG - 1)
    valid = jnp.logical_and(r < counts[e_s], jnp.repeat(s < n_used, MOE_GR))
    slot_tok = jnp.where(valid, order[src] // TOP_K, N_TOK).astype(jnp.int32)
    return gates, slot_tok, slot_of_asg.astype(jnp.int32), g_e.astype(jnp.int32), g_n, s_c.astype(jnp.int32)


LN2_TM = 256


def _ln2_kernel(x_ref, y_ref, g_ref, lw_ref, lb_ref, o_ref):
    g = g_ref[...]
    moe = y_ref[0].astype(F32) * g[:, 0:1]
    for k in range(1, TOP_K):
        moe = moe + y_ref[k].astype(F32) * g[:, k:k + 1]
    o_ref[...] = _layer_norm(DN_ALPHA * x_ref[...] + moe, lw_ref[...], lb_ref[...])


def _ln2(x1, yk, gates, lw, lb):
    return pl.pallas_call(
        _ln2_kernel,
        grid=(N_PAD // LN2_TM,),
        in_specs=[
            pl.BlockSpec((LN2_TM, D_MODEL), lambda i: (i, 0)),
            pl.BlockSpec((TOP_K, LN2_TM, D_MODEL), lambda i: (0, i, 0)),
            pl.BlockSpec((LN2_TM, TOP_K), lambda i: (i, 0)),
            pl.BlockSpec((1, D_MODEL), lambda i: (0, 0)),
            pl.BlockSpec((1, D_MODEL), lambda i: (0, 0)),
        ],
        out_specs=pl.BlockSpec((LN2_TM, D_MODEL), lambda i: (i, 0)),
        out_shape=jax.ShapeDtypeStruct((N_PAD, D_MODEL), F32),
        compiler_params=_cparams(("parallel",)),
        name="combine_ln2",
    )(x1, yk, gates, lw, lb)


def _prep_w_in(w_in, b_sb_qk):
    o_qg, o_kg, o_vg, o_rg, o_a, o_qs = 0, 512, 1024, 2048, 3072, 3088
    o_ks, o_vs = o_qs + SB_W, o_qs + 2 * SB_W
    cols = [
        w_in[:, o_qs:o_qs + SB_W], w_in[:, o_ks:o_ks + SB_W], w_in[:, o_vs:o_vs + SB_W],
        w_in[:, o_qg:o_vg], w_in[:, o_vg:o_rg], w_in[:, o_rg:o_a],
        w_in[:, o_a:o_qs], jnp.zeros((D_MODEL, A_PAD - GLA_RANK), w_in.dtype),
    ]
    w = jnp.concatenate(cols, axis=1).astype(BF16)
    bias = jnp.concatenate([b_sb_qk, jnp.zeros((H_COLS - 2 * SB_W,), F32)]).reshape(1, H_COLS)
    return w, bias


def _prep_router(w_router, b_router):
    wp = jnp.pad(w_router, ((0, 0), (0, R_PAD - N_EXP)))
    hi = wp.astype(BF16)
    lo = (wp - hi.astype(F32)).astype(BF16)
    bp = jnp.pad(b_router, (0, R_PAD - N_EXP)).reshape(1, R_PAD)
    return jnp.concatenate([hi, lo], axis=1), bp


def kernel(x_prompt, x_sample, cache_sb_k, cache_sb_v, state_gla, page_table, w_in, b_sb_qk, w_gla_a2, b_gla_a,
           gla_norm_w, w_o, ln1_w, ln1_b, w_router, b_router, w_up, b_up, w_down, b_down, ln2_w, ln2_b):
    n_pool = cache_sb_k.shape[1]
    cache_k = cache_sb_k.reshape(DEPTH, n_pool, PAGE * SB_H, SB_HD)
    cache_v = cache_sb_v.reshape(DEPTH, n_pool, PAGE * SB_H, SB_HD)
    b_up4 = b_up.reshape(DEPTH, N_EXP, 1, 2 * D_EXP)
    b_down4 = b_down.reshape(DEPTH, N_EXP, 1, D_MODEL)
    x = jnp.concatenate([x_prompt.reshape(N_PROMPT, D_MODEL), x_sample.reshape(N_SAMPLE, D_MODEL),
                         jnp.zeros((N_PAD - N_TOK, D_MODEL), F32)], axis=0)
    kp, vp, gp, ks, vs, gs = [], [], [], [], [], []
    for l in range(DEPTH):
        w1, bias1 = _prep_w_in(w_in[l], b_sb_qk[l])
        hmat = _inproj(x, w1, bias1)
        wa = jnp.pad(w_gla_a2[l], ((0, A_PAD - GLA_RANK), (0, 0))).astype(BF16)
        ba = b_gla_a[l].reshape(1, GLA_QK)
        nw = gla_norm_w[l].reshape(1, GLA_W)
        o_g, g_p = _gla_prompt(hmat, wa, ba, nw)
        o_g, g_s = _gla_sample(hmat, wa, ba, nw, state_gla[l], o_g)
        o_s = _sb_prompt(hmat)
        o_s = _sb_sample(hmat, cache_k, cache_v, page_table, l, o_s)
        wr, br = _prep_router(w_router[l], b_router[l])
        x1, x1b, logits = _outproj(o_g, o_s, x, w_o[l].astype(BF16), ln1_w[l].reshape(1, D_MODEL),
                                   ln1_b[l].reshape(1, D_MODEL), wr, br)
        gates, slot_tok, slot_of_asg, g_e, g_n, g_x = _route(logits[:N_TOK, :N_EXP])
        xs = jnp.take(x1b, slot_tok, axis=0, mode="clip")
        ys = _moe_experts(g_e, g_n, g_x, xs, w_up, b_up4, w_down, b_down4, l)
        slot_k = jnp.pad(slot_of_asg.reshape(N_TOK, TOP_K), ((0, N_PAD - N_TOK), (0, 0))).T
        yk = jnp.take(ys, slot_k, axis=0, mode="clip")
        gates_p = jnp.pad(gates, ((0, N_PAD - N_TOK), (0, 0)))
        x = _ln2(x1, yk, gates_p, ln2_w[l].reshape(1, D_MODEL), ln2_b[l].reshape(1, D_MODEL))
        kp.append(hmat[:N_PROMPT, C_KS:C_KS + SB_W].reshape(B_P, T_P, SB_H, SB_HD))
        vp.append(hmat[:N_PROMPT, C_VS:C_VS + SB_W].reshape(B_P, T_P, SB_H, SB_HD))
        ks.append(hmat[N_PROMPT:N_TOK, C_KS:C_KS + SB_W].reshape(B_S, T_S, SB_H, SB_HD))
        vs.append(hmat[N_PROMPT:N_TOK, C_VS:C_VS + SB_W].reshape(B_S, T_S, SB_H, SB_HD))
        gp.append(g_p)
        gs.append(g_s)
    return (x[:N_PROMPT].reshape(B_P, T_P, D_MODEL), x[N_PROMPT:N_TOK].reshape(B_S, T_S, D_MODEL),
            jnp.stack(kp), jnp.stack(vp), jnp.stack(gp), jnp.stack(ks), jnp.stack(vs), jnp.stack(gs))
```

```python
import functools

import jax
import jax.numpy as jnp
from jax import lax
from jax.experimental import pallas as pl
from jax.experimental.pallas import tpu as pltpu

F32 = jnp.float32
BF16 = jnp.bfloat16

D_MODEL = 2048
DEPTH = 4
B_P, T_P = 2, 4096
B_S, T_S = 8, 8
N_PROMPT = B_P * T_P
N_SAMPLE = B_S * T_S
N_TOK = N_PROMPT + N_SAMPLE
N_PAD = 8448
PAGE = 128

GLA_H, GLA_DK, GLA_DV = 4, 128, 256
GLA_QK = GLA_H * GLA_DK
GLA_W = GLA_H * GLA_DV
GLA_RANK = 16
GLA_TAU = 16.0
GLA_C = 64

SB_H, SB_HD = 8, 128
SB_W = SB_H * SB_HD

N_EXP = 32
TOP_K = 4
D_EXP = D_MODEL
SWIGLU_LIMIT = 7.0
SWIGLU_ALPHA = 1.702
LN_EPS = 1e-5
DN_ALPHA = (2 * DEPTH) ** 0.25

C_QS, C_KS, C_VS, C_QKG, C_VG, C_RG, C_A = 0, 1024, 2048, 3072, 4096, 5120, 6144
H_COLS = 6272
A_PAD = 128

VMEM_LIMIT = 56 * 1024 * 1024


def _cparams(sem):
    return pltpu.CompilerParams(dimension_semantics=sem, vmem_limit_bytes=VMEM_LIMIT)


def _dot(a, b):
    return jnp.dot(a, b, preferred_element_type=F32)


def _dot_nt(a, b):
    return lax.dot_general(a, b, (((1,), (1,)), ((), ())), preferred_element_type=F32)


def _dot_tn(a, b):
    return lax.dot_general(a, b, (((0,), (0,)), ((), ())), preferred_element_type=F32)


def _split_bf16(x):
    hi = x.astype(BF16)
    lo = (x - hi.astype(F32)).astype(BF16)
    return hi, lo


def _softplus(z):
    return jnp.maximum(z, 0.0) + jnp.log(1.0 + jnp.exp(-jnp.abs(z)))


def _log_sigmoid(x):
    return jnp.minimum(x, 0.0) - jnp.log(1.0 + jnp.exp(-jnp.abs(x)))


def _sigmoid(x):
    return 1.0 / (1.0 + jnp.exp(-x))


def _iota(shape, dim):
    return lax.broadcasted_iota(jnp.int32, shape, dim)


IN_TM, IN_TN = 768, 896


def _inproj_kernel(x_ref, w_ref, b_ref, o_ref, xb_ref):
    @pl.when(pl.program_id(1) == 0)
    def _():
        xb_ref[...] = x_ref[...].astype(BF16)

    o_ref[...] = _dot(xb_ref[...], w_ref[...]) + b_ref[...]


def _inproj(x, w, b):
    return pl.pallas_call(
        _inproj_kernel,
        grid=(N_PAD // IN_TM, H_COLS // IN_TN),
        in_specs=[
            pl.BlockSpec((IN_TM, D_MODEL), lambda i, j: (i, 0)),
            pl.BlockSpec((D_MODEL, IN_TN), lambda i, j: (0, j)),
            pl.BlockSpec((1, IN_TN), lambda i, j: (0, j)),
        ],
        out_specs=pl.BlockSpec((IN_TM, IN_TN), lambda i, j: (i, j)),
        out_shape=jax.ShapeDtypeStruct((N_PAD, H_COLS), F32),
        scratch_shapes=[pltpu.VMEM((IN_TM, D_MODEL), BF16)],
        compiler_params=_cparams(("parallel", "arbitrary")),
        name="inproj",
    )(x, w, b)


def _gla_chunk(qk, vv, rr, a, wa, ba, nw, s_list, valid):
    c = GLA_C
    half = c // 2
    pre = _dot(a.astype(BF16), wa) + ba
    la = _log_sigmoid(pre) * (1.0 / GLA_TAU)
    if valid is not None:
        la = jnp.where(valid, la, 0.0)
    la_hi, la_lo = _split_bf16(la)
    tril = _iota((c, c), 0) >= _iota((c, c), 1)
    ltri = jnp.where(tril, 1.0, 0.0).astype(BF16)
    bcum = _dot(ltri, la_hi) + _dot(ltri, la_lo)
    tril_h = _iota((half, half), 0) >= _iota((half, half), 1)

    def diag_scores(qx, kx, bx):
        m = bx[half // 2:half // 2 + 1, :]
        s = _dot_nt((qx * jnp.exp(bx - m)).astype(BF16), (kx * jnp.exp(m - bx)).astype(BF16))
        return jnp.where(tril_h, s, 0.0)

    outs, s_new = [], []
    for h in range(GLA_H):
        q = qk[:, h * GLA_DK:(h + 1) * GLA_DK] * (GLA_DK ** -0.5)
        k = qk[:, GLA_QK + h * GLA_DK:GLA_QK + (h + 1) * GLA_DK]
        if valid is not None:
            k = jnp.where(valid, k, 0.0)
        bh = bcum[:, h * GLA_DK:(h + 1) * GLA_DK]
        vb = vv[:, h * GLA_DV:(h + 1) * GLA_DV].astype(BF16)
        s_old = s_list[h]
        o = _dot((q * jnp.exp(bh)).astype(BF16), s_old.astype(BF16))
        b_last = bh[c - 1:c, :]
        k_state = (k * jnp.exp(b_last - bh)).astype(BF16)
        r = bh[half - 1:half, :]
        q2 = (q[half:] * jnp.exp(bh[half:] - r)).astype(BF16)
        k1 = (k[:half] * jnp.exp(r - bh[:half])).astype(BF16)
        s21 = _dot_nt(q2, k1)
        s11 = diag_scores(q[:half], k[:half], bh[:half])
        s22 = diag_scores(q[half:], k[half:], bh[half:])
        o1 = _dot(s11.astype(BF16), vb[:half])
        o2 = _dot(s21.astype(BF16), vb[:half]) + _dot(s22.astype(BF16), vb[half:])
        o = o + jnp.concatenate([o1, o2], axis=0)
        dec = jnp.transpose(jnp.broadcast_to(jnp.exp(b_last), (GLA_DK, GLA_DK)))
        s_new.append(jnp.concatenate([dec, dec], axis=1) * s_old + _dot_tn(k_state, vb))
        mu = jnp.mean(o, axis=-1, keepdims=True)
        d = o - mu
        var = jnp.mean(d * d, axis=-1, keepdims=True)
        on = d * lax.rsqrt(var + LN_EPS) * nw[:, h * GLA_DV:(h + 1) * GLA_DV]
        rg = rr[:, h * GLA_DV:(h + 1) * GLA_DV]
        outs.append(on * (rg * _sigmoid(rg)))
    return jnp.concatenate(outs, axis=1), s_new


def _gla_prompt_kernel(qk_ref, v_ref, r_ref, a_ref, wa_ref, ba_ref, nw_ref, o_ref, sfin_ref, s_scr):
    j = pl.program_id(1)

    @pl.when(j == 0)
    def _():
        s_scr[...] = jnp.zeros_like(s_scr)

    out, s_new = _gla_chunk(qk_ref[...], v_ref[...], r_ref[...], a_ref[...], wa_ref[...], ba_ref[...],
                            nw_ref[...], [s_scr[h] for h in range(GLA_H)], None)
    o_ref[...] = out
    for h in range(GLA_H):
        s_scr[h] = s_new[h]

    @pl.when(j == pl.num_programs(1) - 1)
    def _():
        sfin_ref[0] = s_scr[...]


def _gla_prompt(hmat, wa, ba, nw):
    nchunk = T_P // GLA_C
    row = lambda b, j: b * nchunk + j
    return pl.pallas_call(
        _gla_prompt_kernel,
        grid=(B_P, nchunk),
        in_specs=[
            pl.BlockSpec((GLA_C, 1024), lambda b, j: (row(b, j), C_QKG // 1024)),
            pl.BlockSpec((GLA_C, 1024), lambda b, j: (row(b, j), C_VG // 1024)),
            pl.BlockSpec((GLA_C, 1024), lambda b, j: (row(b, j), C_RG // 1024)),
            pl.BlockSpec((GLA_C, A_PAD), lambda b, j: (row(b, j), C_A // A_PAD)),
            pl.BlockSpec((A_PAD, GLA_QK), lambda b, j: (0, 0)),
            pl.BlockSpec((1, GLA_QK), lambda b, j: (0, 0)),
            pl.BlockSpec((1, GLA_W), lambda b, j: (0, 0)),
        ],
        out_specs=[
            pl.BlockSpec((GLA_C, GLA_W), lambda b, j: (row(b, j), 0)),
            pl.BlockSpec((1, GLA_H, GLA_DK, GLA_DV), lambda b, j: (b, 0, 0, 0)),
        ],
        out_shape=[
            jax.ShapeDtypeStruct((N_PAD, GLA_W), F32),
            jax.ShapeDtypeStruct((B_P, GLA_H, GLA_DK, GLA_DV), F32),
        ],
        scratch_shapes=[pltpu.VMEM((GLA_H, GLA_DK, GLA_DV), F32)],
        compiler_params=_cparams(("parallel", "arbitrary")),
        name="gla_prompt",
    )(hmat, hmat, hmat, hmat, wa, ba, nw)


TAIL = N_PAD - N_PROMPT


def _pad_rows(x, rows):
    return jnp.concatenate([x, jnp.zeros((rows - x.shape[0], x.shape[1]), x.dtype)], axis=0)


def _gla_sample_kernel(qk_ref, v_ref, r_ref, a_ref, wa_ref, ba_ref, nw_ref, s0_ref, oin_ref, o_ref, sfin_ref):
    del oin_ref
    b = pl.program_id(0)

    @pl.when(b == 0)
    def _():
        o_ref[...] = jnp.zeros_like(o_ref)

    valid = _iota((GLA_C, 1), 0) < T_S
    out, s_new = _gla_chunk(_pad_rows(qk_ref[...], GLA_C), _pad_rows(v_ref[...], GLA_C),
                            _pad_rows(r_ref[...], GLA_C), _pad_rows(a_ref[...], GLA_C),
                            wa_ref[...], ba_ref[...], nw_ref[...],
                            [s0_ref[0, h] for h in range(GLA_H)], valid)
    o_ref[pl.ds(pl.multiple_of(b * T_S, T_S), T_S), :] = out[:T_S]
    for h in range(GLA_H):
        sfin_ref[0, h] = s_new[h]


def _gla_sample(hmat, wa, ba, nw, s0, o_g):
    row = lambda b: N_PROMPT // T_S + b
    return pl.pallas_call(
        _gla_sample_kernel,
        grid=(B_S,),
        in_specs=[
            pl.BlockSpec((T_S, 1024), lambda b: (row(b), C_QKG // 1024)),
            pl.BlockSpec((T_S, 1024), lambda b: (row(b), C_VG // 1024)),
            pl.BlockSpec((T_S, 1024), lambda b: (row(b), C_RG // 1024)),
            pl.BlockSpec((T_S, A_PAD), lambda b: (row(b), C_A // A_PAD)),
            pl.BlockSpec((A_PAD, GLA_QK), lambda b: (0, 0)),
            pl.BlockSpec((1, GLA_QK), lambda b: (0, 0)),
            pl.BlockSpec((1, GLA_W), lambda b: (0, 0)),
            pl.BlockSpec((1, GLA_H, GLA_DK, GLA_DV), lambda b: (b, 0, 0, 0)),
            pl.BlockSpec(memory_space=pl.ANY),
        ],
        out_specs=[
            pl.BlockSpec((TAIL, GLA_W), lambda b: (N_PROMPT // TAIL, 0)),
            pl.BlockSpec((1, GLA_H, GLA_DK, GLA_DV), lambda b: (b, 0, 0, 0)),
        ],
        out_shape=[
            jax.ShapeDtypeStruct((N_PAD, GLA_W), F32),
            jax.ShapeDtypeStruct((B_S, GLA_H, GLA_DK, GLA_DV), F32),
        ],
        input_output_aliases={8: 0},
        compiler_params=_cparams(("arbitrary",)),
        name="gla_sample",
    )(hmat, hmat, hmat, hmat, wa, ba, nw, s0, o_g)


SBP_T = 256
SBP_NH = 2


def _sb_weights(z, carry, utri, live):
    sp = _softplus(z)
    if live is not None:
        sp = jnp.where(live, sp, 0.0)
    hi, lo = _split_bf16(sp)
    rin = _dot(hi, utri) + _dot(lo, utri)
    w = jnp.exp(z - (rin + carry))
    if live is not None:
        w = jnp.where(live, w, 0.0)
    return w.astype(BF16), carry + rin[:, 0:1]


def _suffix_ones(n):
    return jnp.where(_iota((n, n), 0) >= _iota((n, n), 1), 1.0, 0.0).astype(BF16)


def _sbp_kernel(q_ref, k_ref, v_ref, o_ref, kb_scr, vb_scr):
    qi = pl.program_id(2)

    @pl.when(qi == 0)
    def _():
        kb_scr[...] = k_ref[...].astype(BF16)
        vb_scr[...] = v_ref[...].astype(BF16)

    t = SBP_T
    hs = [slice(h * SB_HD, (h + 1) * SB_HD) for h in range(SBP_NH)]
    qs = [(q_ref[:, hs[h]] * (SB_HD ** -0.5)).astype(BF16) for h in range(SBP_NH)]
    utri = _suffix_ones(t)

    def sweep(kb, state, live):
        sl = pl.ds(pl.multiple_of(kb * t, t), t)
        out = []
        for h in range(SBP_NH):
            carry, acc = state[h]
            w, carry = _sb_weights(_dot_nt(qs[h], kb_scr[sl, hs[h]]), carry, utri, live)
            out.append((carry, acc + _dot(w, vb_scr[sl, hs[h]])))
        return tuple(out)

    zero = (jnp.zeros((t, 1), F32), jnp.zeros((t, SB_HD), F32))
    state = sweep(qi, (zero,) * SBP_NH, _iota((t, t), 1) < _iota((t, t), 0))
    state = lax.fori_loop(0, qi, lambda it, st: sweep(qi - 1 - it, st, None), state)
    for h in range(SBP_NH):
        o_ref[:, hs[h]] = state[h][1]


def _sb_prompt(hmat):
    nq = T_P // SBP_T
    wid = SBP_NH * SB_HD
    return pl.pallas_call(
        _sbp_kernel,
        grid=(B_P, SB_H // SBP_NH, nq),
        in_specs=[
            pl.BlockSpec((SBP_T, wid), lambda b, h, i: (b * nq + i, C_QS // wid + h)),
            pl.BlockSpec((T_P, wid), lambda b, h, i: (b, C_KS // wid + h)),
            pl.BlockSpec((T_P, wid), lambda b, h, i: (b, C_VS // wid + h)),
        ],
        out_specs=pl.BlockSpec((SBP_T, wid), lambda b, h, i: (b * nq + i, h)),
        out_shape=jax.ShapeDtypeStruct((N_PAD, SB_W), F32),
        scratch_shapes=[pltpu.VMEM((T_P, wid), BF16), pltpu.VMEM((T_P, wid), BF16)],
        compiler_params=_cparams(("parallel", "parallel", "arbitrary")),
        name="sb_prompt",
    )(hmat, hmat, hmat)


SBS_PP = 8
SBS_TP = 16
SBS_ROWS = SB_H * SBS_TP


def _sbs_pages(qs, k_head, v_head, npages, carry, acc, utri, live):
    z = jnp.concatenate([_dot_nt(qs[h], k_head(h).astype(BF16)) for h in range(SB_H)], axis=0)
    sp = _softplus(z)
    if live is not None:
        sp = jnp.where(live, sp, 0.0)
    hi, lo = _split_bf16(sp)
    ws = []
    for i in range(npages):
        cols = slice(i * PAGE, (i + 1) * PAGE)
        rin = _dot(hi[:, cols], utri) + _dot(lo[:, cols], utri)
        ws.append(jnp.exp(z[:, cols] - (rin + carry)))
        carry = carry + rin[:, 0:1]
    w = jnp.concatenate(ws, axis=1) if npages > 1 else ws[0]
    if live is not None:
        w = jnp.where(live, w, 0.0)
    w = w.astype(BF16)
    pv = [_dot(w[h * SBS_TP:(h + 1) * SBS_TP], v_head(h).astype(BF16)) for h in range(SB_H)]
    return carry, acc + jnp.concatenate(pv, axis=0)


def _sbs_kernel(pt_ref, q_ref, kn_ref, vn_ref, *rest):
    del pt_ref
    k_refs = rest[:SBS_PP]
    v_refs = rest[SBS_PP:2 * SBS_PP]
    oin_ref, o_ref, q_scr, carry_scr, acc_scr = rest[2 * SBS_PP:]
    del oin_ref
    b = pl.program_id(0)
    j = pl.program_id(1)
    utri = _suffix_ones(PAGE)
    head = lambda h: slice(h * SB_HD, (h + 1) * SB_HD)

    @pl.when(jnp.logical_and(b == 0, j == 0))
    def _():
        o_ref[...] = jnp.zeros_like(o_ref)

    @pl.when(j == 0)
    def _():
        q = _pad_rows(q_ref[...] * (SB_HD ** -0.5), SBS_TP)
        q_scr[...] = jnp.concatenate([q[:, head(h)] for h in range(SB_H)], axis=0).astype(BF16)
        qs = [q[:, head(h)].astype(BF16) for h in range(SB_H)]
        kn = _pad_rows(kn_ref[...], PAGE)
        vn = _pad_rows(vn_ref[...], PAGE)
        live = _iota((SBS_ROWS, PAGE), 1) < (_iota((SBS_ROWS, PAGE), 0) & (SBS_TP - 1))
        carry, acc = _sbs_pages(qs, lambda h: kn[:, head(h)], lambda h: vn[:, head(h)], 1,
                                jnp.zeros((SBS_ROWS, 1), F32), jnp.zeros((SBS_ROWS, SB_HD), F32), utri, live)
        carry_scr[...] = carry
        acc_scr[...] = acc

    qs = [q_scr[h * SBS_TP:(h + 1) * SBS_TP, :] for h in range(SB_H)]

    def head_rows(refs):
        return lambda h: jnp.concatenate([r[pl.ds(h, PAGE, stride=SB_H), :] for r in refs], axis=0)

    carry, acc = _sbs_pages(qs, head_rows(k_refs), head_rows(v_refs), SBS_PP,
                            carry_scr[...], acc_scr[...], utri, None)
    carry_scr[...] = carry
    acc_scr[...] = acc

    @pl.when(j == pl.num_programs(1) - 1)
    def _():
        out = jnp.concatenate([acc[h * SBS_TP:h * SBS_TP + T_S, :] for h in range(SB_H)], axis=1)
        o_ref[pl.ds(pl.multiple_of(b * T_S, T_S), T_S), :] = out


def _sb_sample(hmat, cache_k, cache_v, page_table, layer, o_s):
    n_pages = page_table.shape[1]
    nsteps = n_pages // SBS_PP
    row = lambda b: N_PROMPT // T_S + b

    def page_map(i):
        return lambda b, j, pt: (layer, pt[b, n_pages - 1 - (j * SBS_PP + i)], 0, 0)

    page_spec = lambda i: pl.BlockSpec((None, None, PAGE * SB_H, SB_HD), page_map(i))
    grid_spec = pltpu.PrefetchScalarGridSpec(
        num_scalar_prefetch=1,
        grid=(B_S, nsteps),
        in_specs=[
            pl.BlockSpec((T_S, SB_W), lambda b, j, pt: (row(b), C_QS // SB_W)),
            pl.BlockSpec((T_S, SB_W), lambda b, j, pt: (row(b), C_KS // SB_W)),
            pl.BlockSpec((T_S, SB_W), lambda b, j, pt: (row(b), C_VS // SB_W)),
        ] + [page_spec(i) for i in range(SBS_PP)] + [page_spec(i) for i in range(SBS_PP)] + [
            pl.BlockSpec(memory_space=pl.ANY),
        ],
        out_specs=pl.BlockSpec((TAIL, SB_W), lambda b, j, pt: (N_PROMPT // TAIL, 0)),
        scratch_shapes=[
            pltpu.VMEM((SBS_ROWS, SB_HD), BF16),
            pltpu.VMEM((SBS_ROWS, 1), F32),
            pltpu.VMEM((SBS_ROWS, SB_HD), F32),
        ],
    )
    n_in = 1 + 3 + 2 * SBS_PP
    return pl.pallas_call(
        _sbs_kernel,
        grid_spec=grid_spec,
        out_shape=jax.ShapeDtypeStruct((N_PAD, SB_W), F32),
        input_output_aliases={n_in: 0},
        compiler_params=_cparams(("arbitrary", "arbitrary")),
        name="sb_sample",
    )(page_table, hmat, hmat, hmat, *([cache_k] * SBS_PP), *([cache_v] * SBS_PP), o_s)


OP_TM = 256
R_PAD = 128


def _layer_norm(y, w, b):
    mu = jnp.mean(y, axis=-1, keepdims=True)
    d = y - mu
    var = jnp.mean(d * d, axis=-1, keepdims=True)
    return d * lax.rsqrt(var + LN_EPS) * w + b


def _outproj_kernel(og_ref, os_ref, x_ref, wo_ref, lw_ref, lb_ref, wr_ref, br_ref, x1_ref, xb_ref, lg_ref):
    mix = _dot(og_ref[...].astype(BF16), wo_ref[:GLA_W, :]) + _dot(os_ref[...].astype(BF16), wo_ref[GLA_W:, :])
    x1 = _layer_norm(DN_ALPHA * x_ref[...] + mix, lw_ref[...], lb_ref[...])
    x1_ref[...] = x1
    xb_ref[...] = x1.astype(BF16)
    hi, lo = _split_bf16(x1)
    p = _dot(hi, wr_ref[...])
    q = _dot(lo, wr_ref[:, :R_PAD])
    lg_ref[...] = p[:, :R_PAD] + p[:, R_PAD:] + q + br_ref[...]


def _outproj(o_g, o_s, x, wo, lw, lb, wr, br):
    row = lambda i: (i, 0)
    fix = lambda i: (0, 0)
    return pl.pallas_call(
        _outproj_kernel,
        grid=(N_PAD // OP_TM,),
        in_specs=[
            pl.BlockSpec((OP_TM, GLA_W), row),
            pl.BlockSpec((OP_TM, SB_W), row),
            pl.BlockSpec((OP_TM, D_MODEL), row),
            pl.BlockSpec((D_MODEL, D_MODEL), fix),
            pl.BlockSpec((1, D_MODEL), fix),
            pl.BlockSpec((1, D_MODEL), fix),
            pl.BlockSpec((D_MODEL, 2 * R_PAD), fix),
            pl.BlockSpec((1, R_PAD), fix),
        ],
        out_specs=[
            pl.BlockSpec((OP_TM, D_MODEL), row),
            pl.BlockSpec((OP_TM, D_MODEL), row),
            pl.BlockSpec((OP_TM, R_PAD), row),
        ],
        out_shape=[
            jax.ShapeDtypeStruct((N_PAD, D_MODEL), F32),
            jax.ShapeDtypeStruct((2 * N_PAD, D_MODEL), BF16),
            jax.ShapeDtypeStruct((N_PAD, R_PAD), F32),
        ],
        compiler_params=_cparams(("parallel",)),
        name="outproj_ln1",
    )(o_g, o_s, x, wo, lw, lb, wr, br)


MOE_BM = 256
MOE_CAP = 5
MOE_GR = MOE_BM * MOE_CAP
MOE_TF = 512
N_ASG = N_TOK * TOP_K
MOE_NG = N_ASG // MOE_GR + N_EXP
MOE_NSLOT = MOE_NG * MOE_GR
MOE_NCH = D_EXP // MOE_TF


def _moe_kernel(ge_ref, gn_ref, gx_ref, x_ref, wg_ref, wu_ref, bg_ref, bu_ref, wd_ref, bd_ref, y_ref,
                wgb, wub, wdb, acc):
    del ge_ref, gx_ref
    c = pl.program_id(1)
    n = gn_ref[pl.program_id(0)]
    last = c == pl.num_programs(1) - 1

    @pl.when(n > 0)
    def _():
        wgb[...] = wg_ref[...].astype(BF16)
        wub[...] = wu_ref[...].astype(BF16)
        wdb[...] = wd_ref[...].astype(BF16)

    for i in range(MOE_CAP):
        rows = slice(i * MOE_BM, (i + 1) * MOE_BM)

        @pl.when(jnp.logical_and(i < n, c == 0))
        def _():
            acc[rows, :] = jnp.broadcast_to(bd_ref[...], (MOE_BM, D_MODEL))

        @pl.when(i < n)
        def _():
            x = x_ref[rows, :]
            g = jnp.minimum(_dot(x, wgb[...]) + bg_ref[...], SWIGLU_LIMIT)
            u = jnp.clip(_dot(x, wub[...]) + bu_ref[...], -SWIGLU_LIMIT, SWIGLU_LIMIT)
            act = (u + 1.0) * (g * _sigmoid(SWIGLU_ALPHA * g))
            acc[rows, :] += _dot(act.astype(BF16), wdb[...])

        @pl.when(jnp.logical_and(i < n, last))
        def _():
            y_ref[rows, :] = acc[rows, :].astype(BF16)

        @pl.when(jnp.logical_and(jnp.logical_and(i >= n, n > 0), last))
        def _():
            y_ref[rows, :] = jnp.zeros((MOE_BM, D_MODEL), BF16)


def _moe_experts(g_e, g_n, g_x, xs, w_up, b_up, w_down, b_down, layer):
    chunk = lambda s, c, gn: jnp.where(gn[s] > 0, c, MOE_NCH - 1)
    grid_spec = pltpu.PrefetchScalarGridSpec(
        num_scalar_prefetch=3,
        grid=(MOE_NG, MOE_NCH),
        in_specs=[
            pl.BlockSpec((MOE_GR, D_MODEL), lambda s, c, ge, gn, gx: (gx[s], 0), pipeline_mode=pl.Buffered(1)),
            pl.BlockSpec((None, None, D_MODEL, MOE_TF), lambda s, c, ge, gn, gx: (layer, ge[s], 0, chunk(s, c, gn))),
            pl.BlockSpec((None, None, D_MODEL, MOE_TF),
                         lambda s, c, ge, gn, gx: (layer, ge[s], 0, MOE_NCH + chunk(s, c, gn))),
            pl.BlockSpec((None, None, 1, MOE_TF), lambda s, c, ge, gn, gx: (layer, ge[s], 0, chunk(s, c, gn))),
            pl.BlockSpec((None, None, 1, MOE_TF),
                         lambda s, c, ge, gn, gx: (layer, ge[s], 0, MOE_NCH + chunk(s, c, gn))),
            pl.BlockSpec((None, None, MOE_TF, D_MODEL), lambda s, c, ge, gn, gx: (layer, ge[s], chunk(s, c, gn), 0)),
            pl.BlockSpec((None, None, 1, D_MODEL), lambda s, c, ge, gn, gx: (layer, ge[s], 0, 0)),
        ],
        out_specs=pl.BlockSpec((MOE_GR, D_MODEL), lambda s, c, ge, gn, gx: (gx[s], 0), pipeline_mode=pl.Buffered(1)),
        scratch_shapes=[
            pltpu.VMEM((D_MODEL, MOE_TF), BF16),
            pltpu.VMEM((D_MODEL, MOE_TF), BF16),
            pltpu.VMEM((MOE_TF, D_MODEL), BF16),
            pltpu.VMEM((MOE_GR, D_MODEL), F32),
        ],
    )
    return pl.pallas_call(
        _moe_kernel,
        grid_spec=grid_spec,
        out_shape=jax.ShapeDtypeStruct((MOE_NSLOT, D_MODEL), BF16),
        compiler_params=_cparams(("arbitrary", "arbitrary")),
        name="moe_experts",
    )(g_e, g_n, g_x, xs, w_up, w_up, b_up, b_up, w_down, b_down)


def _route(logits):
    top_val, top_idx = lax.top_k(logits, TOP_K)
    gates = jax.nn.softmax(top_val, axis=-1)
    flat_e = top_idx.reshape(-1).astype(jnp.int32)
    asg = jnp.arange(N_ASG, dtype=jnp.int32)
    _, order = lax.sort((flat_e, asg), num_keys=1)
    _, rank = lax.sort((order, asg), num_keys=1)
    experts = jnp.arange(N_EXP, dtype=jnp.int32)
    counts = jnp.sum((flat_e[:, None] == experts[None, :]).astype(jnp.int32), axis=0)
    starts = jnp.cumsum(counts) - counts
    ng = (counts + MOE_GR - 1) // MOE_GR
    g_end = jnp.cumsum(ng)
    g_first = g_end - ng
    row0 = g_first * MOE_GR
    n_used = g_end[-1]
    s = jnp.arange(MOE_NG, dtype=jnp.int32)
    s_c = jnp.minimum(s, n_used - 1)
    g_e = jnp.sum((g_end[None, :] <= s_c[:, None]).astype(jnp.int32), axis=1)
    left = counts[g_e] - (s_c - g_first[g_e]) * MOE_GR
    g_n = jnp.where(s < n_used, jnp.clip((left + MOE_BM - 1) // MOE_BM, 0, MOE_CAP), 0).astype(jnp.int32)
    shift = row0 - starts
    slot_of_asg = rank + jnp.sum(jnp.where(flat_e[:, None] == experts[None, :], shift[None, :], 0), axis=1)
    r = (s_c - g_first[g_e])[:, None] * MOE_GR + jnp.arange(MOE_GR, dtype=jnp.int32)[None, :]
    src = jnp.clip(starts[g_e][:, None] + r, 0, N_ASG - 1)
    valid = jnp.logical_and(r < counts[g_e][:, None], (s < n_used)[:, None])
    slot_tok = jnp.where(valid, order[src] // TOP_K, N_TOK).astype(jnp.int32).reshape(MOE_NSLOT)
    return gates, slot_tok, slot_of_asg.astype(jnp.int32), g_e.astype(jnp.int32), g_n, s_c.astype(jnp.int32)


LN2_TM = 256


def _ln2_kernel(x_ref, y_ref, g_ref, lw_ref, lb_ref, o_ref):
    g = g_ref[...]
    moe = y_ref[0].astype(F32) * g[:, 0:1]
    for k in range(1, TOP_K):
        moe = moe + y_ref[k].astype(F32) * g[:, k:k + 1]
    o_ref[...] = _layer_norm(DN_ALPHA * x_ref[...] + moe, lw_ref[...], lb_ref[...])


def _ln2(x1, yk, gates, lw, lb):
    return pl.pallas_call(
        _ln2_kernel,
        grid=(N_PAD // LN2_TM,),
        in_specs=[
            pl.BlockSpec((LN2_TM, D_MODEL), lambda i: (i, 0)),
            pl.BlockSpec((TOP_K, LN2_TM, D_MODEL), lambda i: (0, i, 0)),
            pl.BlockSpec((LN2_TM, TOP_K), lambda i: (i, 0)),
            pl.BlockSpec((1, D_MODEL), lambda i: (0, 0)),
            pl.BlockSpec((1, D_MODEL), lambda i: (0, 0)),
        ],
        out_specs=pl.BlockSpec((LN2_TM, D_MODEL), lambda i: (i, 0)),
        out_shape=jax.ShapeDtypeStruct((N_PAD, D_MODEL), F32),
        compiler_params=_cparams(("parallel",)),
        name="combine_ln2",
    )(x1, yk, gates, lw, lb)


def _prep_w_in(w_in, b_sb_qk):
    o_qg, o_kg, o_vg, o_rg, o_a, o_qs = 0, 512, 1024, 2048, 3072, 3088
    o_ks, o_vs = o_qs + SB_W, o_qs + 2 * SB_W
    cols = [
        w_in[:, o_qs:o_qs + SB_W], w_in[:, o_ks:o_ks + SB_W], w_in[:, o_vs:o_vs + SB_W],
        w_in[:, o_qg:o_vg], w_in[:, o_vg:o_rg], w_in[:, o_rg:o_a],
        w_in[:, o_a:o_qs], jnp.zeros((D_MODEL, A_PAD - GLA_RANK), w_in.dtype),
    ]
    w = jnp.concatenate(cols, axis=1).astype(BF16)
    bias = jnp.concatenate([b_sb_qk, jnp.zeros((H_COLS - 2 * SB_W,), F32)]).reshape(1, H_COLS)
    return w, bias


def _prep_router(w_router, b_router):
    wp = jnp.pad(w_router, ((0, 0), (0, R_PAD - N_EXP)))
    hi = wp.astype(BF16)
    lo = (wp - hi.astype(F32)).astype(BF16)
    bp = jnp.pad(b_router, (0, R_PAD - N_EXP)).reshape(1, R_PAD)
    return jnp.concatenate([hi, lo], axis=1), bp


def kernel(x_prompt, x_sample, cache_sb_k, cache_sb_v, state_gla, page_table, w_in, b_sb_qk, w_gla_a2, b_gla_a,
           gla_norm_w, w_o, ln1_w, ln1_b, w_router, b_router, w_up, b_up, w_down, b_down, ln2_w, ln2_b):
    n_pool = cache_sb_k.shape[1]
    cache_k = cache_sb_k.reshape(DEPTH, n_pool, PAGE * SB_H, SB_HD)
    cache_v = cache_sb_v.reshape(DEPTH, n_pool, PAGE * SB_H, SB_HD)
    b_up4 = b_up.reshape(DEPTH, N_EXP, 1, 2 * D_EXP)
    b_down4 = b_down.reshape(DEPTH, N_EXP, 1, D_MODEL)
    x = jnp.concatenate([x_prompt.reshape(N_PROMPT, D_MODEL), x_sample.reshape(N_SAMPLE, D_MODEL),
                         jnp.zeros((N_PAD - N_TOK, D_MODEL), F32)], axis=0)
    kp, vp, gp, ks, vs, gs = [], [], [], [], [], []
    for l in range(DEPTH):
        w1, bias1 = _prep_w_in(w_in[l], b_sb_qk[l])
        hmat = _inproj(x, w1, bias1)
        wa = jnp.pad(w_gla_a2[l], ((0, A_PAD - GLA_RANK), (0, 0))).astype(BF16)
        ba = b_gla_a[l].reshape(1, GLA_QK)
        nw = gla_norm_w[l].reshape(1, GLA_W)
        o_g, g_p = _gla_prompt(hmat, wa, ba, nw)
        o_g, g_s = _gla_sample(hmat, wa, ba, nw, state_gla[l], o_g)
        o_s = _sb_prompt(hmat)
        o_s = _sb_sample(hmat, cache_k, cache_v, page_table, l, o_s)
        wr, br = _prep_router(w_router[l], b_router[l])
        x1, x1b, logits = _outproj(o_g, o_s, x, w_o[l].astype(BF16), ln1_w[l].reshape(1, D_MODEL),
                                   ln1_b[l].reshape(1, D_MODEL), wr, br)
        gates, slot_tok, slot_of_asg, g_e, g_n, g_x = _route(logits[:N_TOK, :N_EXP])
        xs = x1b.at[slot_tok].get(mode="promise_in_bounds")
        ys = _moe_experts(g_e, g_n, g_x, xs, w_up, b_up4, w_down, b_down4, l)
        slot_k = jnp.pad(slot_of_asg.reshape(N_TOK, TOP_K), ((0, N_PAD - N_TOK), (0, 0))).T
        yk = ys.at[slot_k].get(mode="promise_in_bounds")
        gates_p = jnp.pad(gates, ((0, N_PAD - N_TOK), (0, 0)))
        x = _ln2(x1, yk, gates_p, ln2_w[l].reshape(1, D_MODEL), ln2_b[l].reshape(1, D_MODEL))
        kp.append(hmat[:N_PROMPT, C_KS:C_KS + SB_W].reshape(B_P, T_P, SB_H, SB_HD))
        vp.append(hmat[:N_PROMPT, C_VS:C_VS + SB_W].reshape(B_P, T_P, SB_H, SB_HD))
        ks.append(hmat[N_PROMPT:N_TOK, C_KS:C_KS + SB_W].reshape(B_S, T_S, SB_H, SB_HD))
        vs.append(hmat[N_PROMPT:N_TOK, C_VS:C_VS + SB_W].reshape(B_S, T_S, SB_H, SB_HD))
        gp.append(g_p)
        gs.append(g_s)
    return (x[:N_PROMPT].reshape(B_P, T_P, D_MODEL), x[N_PROMPT:N_TOK].reshape(B_S, T_S, D_MODEL),
            jnp.stack(kp), jnp.stack(vp), jnp.stack(gp), jnp.stack(ks), jnp.stack(vs), jnp.stack(gs))
```

```python
import functools

import jax
import jax.numpy as jnp
from jax import lax
from jax.experimental import pallas as pl
from jax.experimental.pallas import tpu as pltpu

F32 = jnp.float32
BF16 = jnp.bfloat16

D_MODEL = 2048
DEPTH = 4
B_P, T_P = 2, 4096
B_S, T_S = 8, 8
N_PROMPT = B_P * T_P
N_SAMPLE = B_S * T_S
N_TOK = N_PROMPT + N_SAMPLE
N_PAD = 8448
PAGE = 128

GLA_H, GLA_DK, GLA_DV = 4, 128, 256
GLA_QK = GLA_H * GLA_DK
GLA_W = GLA_H * GLA_DV
GLA_RANK = 16
GLA_TAU = 16.0
GLA_C = 64

SB_H, SB_HD = 8, 128
SB_W = SB_H * SB_HD

N_EXP = 32
TOP_K = 4
D_EXP = D_MODEL
SWIGLU_LIMIT = 7.0
SWIGLU_ALPHA = 1.702
LN_EPS = 1e-5
DN_ALPHA = (2 * DEPTH) ** 0.25

C_QS, C_KS, C_VS, C_QKG, C_VG, C_RG, C_A = 0, 1024, 2048, 3072, 4096, 5120, 6144
H_COLS = 6272
A_PAD = 128

VMEM_LIMIT = 56 * 1024 * 1024


def _cparams(sem):
    return pltpu.CompilerParams(dimension_semantics=sem, vmem_limit_bytes=VMEM_LIMIT)


def _dot(a, b):
    return jnp.dot(a, b, preferred_element_type=F32)


def _dot_nt(a, b):
    return lax.dot_general(a, b, (((1,), (1,)), ((), ())), preferred_element_type=F32)


def _dot_tn(a, b):
    return lax.dot_general(a, b, (((0,), (0,)), ((), ())), preferred_element_type=F32)


def _split_bf16(x):
    hi = x.astype(BF16)
    lo = (x - hi.astype(F32)).astype(BF16)
    return hi, lo


def _softplus(z):
    return jnp.maximum(z, 0.0) + jnp.log(1.0 + jnp.exp(-jnp.abs(z)))


def _log_sigmoid(x):
    return jnp.minimum(x, 0.0) - jnp.log(1.0 + jnp.exp(-jnp.abs(x)))


def _sigmoid(x):
    return 1.0 / (1.0 + jnp.exp(-x))


def _iota(shape, dim):
    return lax.broadcasted_iota(jnp.int32, shape, dim)


IN_TM, IN_TN = 768, 896


def _inproj_kernel(x_ref, w_ref, b_ref, o_ref, xb_ref):
    @pl.when(pl.program_id(1) == 0)
    def _():
        xb_ref[...] = x_ref[...].astype(BF16)

    o_ref[...] = _dot(xb_ref[...], w_ref[...]) + b_ref[...]


def _inproj(x, w, b):
    return pl.pallas_call(
        _inproj_kernel,
        grid=(N_PAD // IN_TM, H_COLS // IN_TN),
        in_specs=[
            pl.BlockSpec((IN_TM, D_MODEL), lambda i, j: (i, 0)),
            pl.BlockSpec((D_MODEL, IN_TN), lambda i, j: (0, j)),
            pl.BlockSpec((1, IN_TN), lambda i, j: (0, j)),
        ],
        out_specs=pl.BlockSpec((IN_TM, IN_TN), lambda i, j: (i, j)),
        out_shape=jax.ShapeDtypeStruct((N_PAD, H_COLS), F32),
        scratch_shapes=[pltpu.VMEM((IN_TM, D_MODEL), BF16)],
        compiler_params=_cparams(("parallel", "arbitrary")),
        name="inproj",
    )(x, w, b)


def _gla_chunk(qk, vv, rr, a, wa, ba, nw, s_list, valid):
    c = GLA_C
    half = c // 2
    pre = _dot(a.astype(BF16), wa) + ba
    la = _log_sigmoid(pre) * (1.0 / GLA_TAU)
    if valid is not None:
        la = jnp.where(valid, la, 0.0)
    la_hi, la_lo = _split_bf16(la)
    tril = _iota((c, c), 0) >= _iota((c, c), 1)
    ltri = jnp.where(tril, 1.0, 0.0).astype(BF16)
    bcum = _dot(ltri, la_hi) + _dot(ltri, la_lo)
    tril_h = _iota((half, half), 0) >= _iota((half, half), 1)

    def diag_scores(qx, kx, bx):
        m = bx[half // 2:half // 2 + 1, :]
        s = _dot_nt((qx * jnp.exp(bx - m)).astype(BF16), (kx * jnp.exp(m - bx)).astype(BF16))
        return jnp.where(tril_h, s, 0.0)

    outs, s_new = [], []
    for h in range(GLA_H):
        q = qk[:, h * GLA_DK:(h + 1) * GLA_DK] * (GLA_DK ** -0.5)
        k = qk[:, GLA_QK + h * GLA_DK:GLA_QK + (h + 1) * GLA_DK]
        if valid is not None:
            k = jnp.where(valid, k, 0.0)
        bh = bcum[:, h * GLA_DK:(h + 1) * GLA_DK]
        vb = vv[:, h * GLA_DV:(h + 1) * GLA_DV].astype(BF16)
        s_old = s_list[h]
        o = _dot((q * jnp.exp(bh)).astype(BF16), s_old.astype(BF16))
        b_last = bh[c - 1:c, :]
        k_state = (k * jnp.exp(b_last - bh)).astype(BF16)
        r = bh[half - 1:half, :]
        q2 = (q[half:] * jnp.exp(bh[half:] - r)).astype(BF16)
        k1 = (k[:half] * jnp.exp(r - bh[:half])).astype(BF16)
        s21 = _dot_nt(q2, k1)
        s11 = diag_scores(q[:half], k[:half], bh[:half])
        s22 = diag_scores(q[half:], k[half:], bh[half:])
        o1 = _dot(s11.astype(BF16), vb[:half])
        o2 = _dot(s21.astype(BF16), vb[:half]) + _dot(s22.astype(BF16), vb[half:])
        o = o + jnp.concatenate([o1, o2], axis=0)
        dec = jnp.transpose(jnp.broadcast_to(jnp.exp(b_last), (GLA_DK, GLA_DK)))
        s_new.append(jnp.concatenate([dec, dec], axis=1) * s_old + _dot_tn(k_state, vb))
        mu = jnp.mean(o, axis=-1, keepdims=True)
        d = o - mu
        var = jnp.mean(d * d, axis=-1, keepdims=True)
        on = d * lax.rsqrt(var + LN_EPS) * nw[:, h * GLA_DV:(h + 1) * GLA_DV]
        rg = rr[:, h * GLA_DV:(h + 1) * GLA_DV]
        outs.append(on * (rg * _sigmoid(rg)))
    return jnp.concatenate(outs, axis=1), s_new


def _gla_prompt_kernel(qk_ref, v_ref, r_ref, a_ref, wa_ref, ba_ref, nw_ref, o_ref, sfin_ref, s_scr):
    j = pl.program_id(1)

    @pl.when(j == 0)
    def _():
        s_scr[...] = jnp.zeros_like(s_scr)

    out, s_new = _gla_chunk(qk_ref[...], v_ref[...], r_ref[...], a_ref[...], wa_ref[...], ba_ref[...],
                            nw_ref[...], [s_scr[h] for h in range(GLA_H)], None)
    o_ref[...] = out
    for h in range(GLA_H):
        s_scr[h] = s_new[h]

    @pl.when(j == pl.num_programs(1) - 1)
    def _():
        sfin_ref[0] = s_scr[...]


def _gla_prompt(hmat, wa, ba, nw):
    nchunk = T_P // GLA_C
    row = lambda b, j: b * nchunk + j
    return pl.pallas_call(
        _gla_prompt_kernel,
        grid=(B_P, nchunk),
        in_specs=[
            pl.BlockSpec((GLA_C, 1024), lambda b, j: (row(b, j), C_QKG // 1024)),
            pl.BlockSpec((GLA_C, 1024), lambda b, j: (row(b, j), C_VG // 1024)),
            pl.BlockSpec((GLA_C, 1024), lambda b, j: (row(b, j), C_RG // 1024)),
            pl.BlockSpec((GLA_C, A_PAD), lambda b, j: (row(b, j), C_A // A_PAD)),
            pl.BlockSpec((A_PAD, GLA_QK), lambda b, j: (0, 0)),
            pl.BlockSpec((1, GLA_QK), lambda b, j: (0, 0)),
            pl.BlockSpec((1, GLA_W), lambda b, j: (0, 0)),
        ],
        out_specs=[
            pl.BlockSpec((GLA_C, GLA_W), lambda b, j: (row(b, j), 0)),
            pl.BlockSpec((1, GLA_H, GLA_DK, GLA_DV), lambda b, j: (b, 0, 0, 0)),
        ],
        out_shape=[
            jax.ShapeDtypeStruct((N_PAD, GLA_W), F32),
            jax.ShapeDtypeStruct((B_P, GLA_H, GLA_DK, GLA_DV), F32),
        ],
        scratch_shapes=[pltpu.VMEM((GLA_H, GLA_DK, GLA_DV), F32)],
        compiler_params=_cparams(("parallel", "arbitrary")),
        name="gla_prompt",
    )(hmat, hmat, hmat, hmat, wa, ba, nw)


TAIL = N_PAD - N_PROMPT


def _pad_rows(x, rows):
    return jnp.concatenate([x, jnp.zeros((rows - x.shape[0], x.shape[1]), x.dtype)], axis=0)


def _gla_sample_kernel(qk_ref, v_ref, r_ref, a_ref, wa_ref, ba_ref, nw_ref, s0_ref, oin_ref, o_ref, sfin_ref):
    del oin_ref
    b = pl.program_id(0)

    @pl.when(b == 0)
    def _():
        o_ref[...] = jnp.zeros_like(o_ref)

    valid = _iota((GLA_C, 1), 0) < T_S
    out, s_new = _gla_chunk(_pad_rows(qk_ref[...], GLA_C), _pad_rows(v_ref[...], GLA_C),
                            _pad_rows(r_ref[...], GLA_C), _pad_rows(a_ref[...], GLA_C),
                            wa_ref[...], ba_ref[...], nw_ref[...],
                            [s0_ref[0, h] for h in range(GLA_H)], valid)
    o_ref[pl.ds(pl.multiple_of(b * T_S, T_S), T_S), :] = out[:T_S]
    for h in range(GLA_H):
        sfin_ref[0, h] = s_new[h]


def _gla_sample(hmat, wa, ba, nw, s0, o_g):
    row = lambda b: N_PROMPT // T_S + b
    return pl.pallas_call(
        _gla_sample_kernel,
        grid=(B_S,),
        in_specs=[
            pl.BlockSpec((T_S, 1024), lambda b: (row(b), C_QKG // 1024)),
            pl.BlockSpec((T_S, 1024), lambda b: (row(b), C_VG // 1024)),
            pl.BlockSpec((T_S, 1024), lambda b: (row(b), C_RG // 1024)),
            pl.BlockSpec((T_S, A_PAD), lambda b: (row(b), C_A // A_PAD)),
            pl.BlockSpec((A_PAD, GLA_QK), lambda b: (0, 0)),
            pl.BlockSpec((1, GLA_QK), lambda b: (0, 0)),
            pl.BlockSpec((1, GLA_W), lambda b: (0, 0)),
            pl.BlockSpec((1, GLA_H, GLA_DK, GLA_DV), lambda b: (b, 0, 0, 0)),
            pl.BlockSpec(memory_space=pl.ANY),
        ],
        out_specs=[
            pl.BlockSpec((TAIL, GLA_W), lambda b: (N_PROMPT // TAIL, 0)),
            pl.BlockSpec((1, GLA_H, GLA_DK, GLA_DV), lambda b: (b, 0, 0, 0)),
        ],
        out_shape=[
            jax.ShapeDtypeStruct((N_PAD, GLA_W), F32),
            jax.ShapeDtypeStruct((B_S, GLA_H, GLA_DK, GLA_DV), F32),
        ],
        input_output_aliases={8: 0},
        compiler_params=_cparams(("arbitrary",)),
        name="gla_sample",
    )(hmat, hmat, hmat, hmat, wa, ba, nw, s0, o_g)


SBP_T = 256
SBP_NH = 2


def _sb_weights(z, carry, utri, live):
    sp = _softplus(z)
    if live is not None:
        sp = jnp.where(live, sp, 0.0)
    hi, lo = _split_bf16(sp)
    rin = _dot(hi, utri) + _dot(lo, utri)
    w = jnp.exp(z - (rin + carry))
    if live is not None:
        w = jnp.where(live, w, 0.0)
    return w.astype(BF16), carry + rin[:, 0:1]


def _suffix_ones(n):
    return jnp.where(_iota((n, n), 0) >= _iota((n, n), 1), 1.0, 0.0).astype(BF16)


def _sbp_kernel(q_ref, k_ref, v_ref, o_ref, kb_scr, vb_scr):
    qi = pl.program_id(2)

    @pl.when(qi == 0)
    def _():
        kb_scr[...] = k_ref[...].astype(BF16)
        vb_scr[...] = v_ref[...].astype(BF16)

    t = SBP_T
    hs = [slice(h * SB_HD, (h + 1) * SB_HD) for h in range(SBP_NH)]
    qs = [(q_ref[:, hs[h]] * (SB_HD ** -0.5)).astype(BF16) for h in range(SBP_NH)]
    utri = _suffix_ones(t)

    def sweep(kb, state, live):
        sl = pl.ds(pl.multiple_of(kb * t, t), t)
        out = []
        for h in range(SBP_NH):
            carry, acc = state[h]
            w, carry = _sb_weights(_dot_nt(qs[h], kb_scr[sl, hs[h]]), carry, utri, live)
            out.append((carry, acc + _dot(w, vb_scr[sl, hs[h]])))
        return tuple(out)

    zero = (jnp.zeros((t, 1), F32), jnp.zeros((t, SB_HD), F32))
    state = sweep(qi, (zero,) * SBP_NH, _iota((t, t), 1) < _iota((t, t), 0))
    state = lax.fori_loop(0, qi, lambda it, st: sweep(qi - 1 - it, st, None), state)
    for h in range(SBP_NH):
        o_ref[:, hs[h]] = state[h][1]


def _sb_prompt(hmat):
    nq = T_P // SBP_T
    wid = SBP_NH * SB_HD
    return pl.pallas_call(
        _sbp_kernel,
        grid=(B_P, SB_H // SBP_NH, nq),
        in_specs=[
            pl.BlockSpec((SBP_T, wid), lambda b, h, i: (b * nq + i, C_QS // wid + h)),
            pl.BlockSpec((T_P, wid), lambda b, h, i: (b, C_KS // wid + h)),
            pl.BlockSpec((T_P, wid), lambda b, h, i: (b, C_VS // wid + h)),
        ],
        out_specs=pl.BlockSpec((SBP_T, wid), lambda b, h, i: (b * nq + i, h)),
        out_shape=jax.ShapeDtypeStruct((N_PAD, SB_W), F32),
        scratch_shapes=[pltpu.VMEM((T_P, wid), BF16), pltpu.VMEM((T_P, wid), BF16)],
        compiler_params=_cparams(("parallel", "parallel", "arbitrary")),
        name="sb_prompt",
    )(hmat, hmat, hmat)


SBS_PP = 8
SBS_TP = 16
SBS_ROWS = SB_H * SBS_TP


def _sbs_pages(qs, k_head, v_head, npages, carry, acc, utri, live):
    z = jnp.concatenate([_dot_nt(qs[h], k_head(h).astype(BF16)) for h in range(SB_H)], axis=0)
    sp = _softplus(z)
    if live is not None:
        sp = jnp.where(live, sp, 0.0)
    hi, lo = _split_bf16(sp)
    ws = []
    for i in range(npages):
        cols = slice(i * PAGE, (i + 1) * PAGE)
        rin = _dot(hi[:, cols], utri) + _dot(lo[:, cols], utri)
        ws.append(jnp.exp(z[:, cols] - (rin + carry)))
        carry = carry + rin[:, 0:1]
    w = jnp.concatenate(ws, axis=1) if npages > 1 else ws[0]
    if live is not None:
        w = jnp.where(live, w, 0.0)
    w = w.astype(BF16)
    pv = [_dot(w[h * SBS_TP:(h + 1) * SBS_TP], v_head(h).astype(BF16)) for h in range(SB_H)]
    return carry, acc + jnp.concatenate(pv, axis=0)


def _sbs_kernel(pt_ref, q_ref, kn_ref, vn_ref, *rest):
    del pt_ref
    k_refs = rest[:SBS_PP]
    v_refs = rest[SBS_PP:2 * SBS_PP]
    oin_ref, o_ref, q_scr, carry_scr, acc_scr = rest[2 * SBS_PP:]
    del oin_ref
    b = pl.program_id(0)
    j = pl.program_id(1)
    utri = _suffix_ones(PAGE)
    head = lambda h: slice(h * SB_HD, (h + 1) * SB_HD)

    @pl.when(jnp.logical_and(b == 0, j == 0))
    def _():
        o_ref[...] = jnp.zeros_like(o_ref)

    @pl.when(j == 0)
    def _():
        q = _pad_rows(q_ref[...] * (SB_HD ** -0.5), SBS_TP)
        q_scr[...] = jnp.concatenate([q[:, head(h)] for h in range(SB_H)], axis=0).astype(BF16)
        qs = [q[:, head(h)].astype(BF16) for h in range(SB_H)]
        kn = _pad_rows(kn_ref[...], PAGE)
        vn = _pad_rows(vn_ref[...], PAGE)
        live = _iota((SBS_ROWS, PAGE), 1) < (_iota((SBS_ROWS, PAGE), 0) & (SBS_TP - 1))
        carry, acc = _sbs_pages(qs, lambda h: kn[:, head(h)], lambda h: vn[:, head(h)], 1,
                                jnp.zeros((SBS_ROWS, 1), F32), jnp.zeros((SBS_ROWS, SB_HD), F32), utri, live)
        carry_scr[...] = carry
        acc_scr[...] = acc

    qs = [q_scr[h * SBS_TP:(h + 1) * SBS_TP, :] for h in range(SB_H)]

    def head_rows(refs):
        return lambda h: jnp.concatenate([r[pl.ds(h, PAGE, stride=SB_H), :] for r in refs], axis=0)

    carry, acc = _sbs_pages(qs, head_rows(k_refs), head_rows(v_refs), SBS_PP,
                            carry_scr[...], acc_scr[...], utri, None)
    carry_scr[...] = carry
    acc_scr[...] = acc

    @pl.when(j == pl.num_programs(1) - 1)
    def _():
        out = jnp.concatenate([acc[h * SBS_TP:h * SBS_TP + T_S, :] for h in range(SB_H)], axis=1)
        o_ref[pl.ds(pl.multiple_of(b * T_S, T_S), T_S), :] = out


def _sb_sample(hmat, cache_k, cache_v, page_table, layer, o_s):
    n_pages = page_table.shape[1]
    nsteps = n_pages // SBS_PP
    row = lambda b: N_PROMPT // T_S + b

    def page_map(i):
        return lambda b, j, pt: (layer, pt[b, n_pages - 1 - (j * SBS_PP + i)], 0, 0)

    page_spec = lambda i: pl.BlockSpec((None, None, PAGE * SB_H, SB_HD), page_map(i))
    grid_spec = pltpu.PrefetchScalarGridSpec(
        num_scalar_prefetch=1,
        grid=(B_S, nsteps),
        in_specs=[
            pl.BlockSpec((T_S, SB_W), lambda b, j, pt: (row(b), C_QS // SB_W)),
            pl.BlockSpec((T_S, SB_W), lambda b, j, pt: (row(b), C_KS // SB_W)),
            pl.BlockSpec((T_S, SB_W), lambda b, j, pt: (row(b), C_VS // SB_W)),
        ] + [page_spec(i) for i in range(SBS_PP)] + [page_spec(i) for i in range(SBS_PP)] + [
            pl.BlockSpec(memory_space=pl.ANY),
        ],
        out_specs=pl.BlockSpec((TAIL, SB_W), lambda b, j, pt: (N_PROMPT // TAIL, 0)),
        scratch_shapes=[
            pltpu.VMEM((SBS_ROWS, SB_HD), BF16),
            pltpu.VMEM((SBS_ROWS, 1), F32),
            pltpu.VMEM((SBS_ROWS, SB_HD), F32),
        ],
    )
    n_in = 1 + 3 + 2 * SBS_PP
    return pl.pallas_call(
        _sbs_kernel,
        grid_spec=grid_spec,
        out_shape=jax.ShapeDtypeStruct((N_PAD, SB_W), F32),
        input_output_aliases={n_in: 0},
        compiler_params=_cparams(("arbitrary", "arbitrary")),
        name="sb_sample",
    )(page_table, hmat, hmat, hmat, *([cache_k] * SBS_PP), *([cache_v] * SBS_PP), o_s)


OP_TM = 256
R_PAD = 128


def _layer_norm(y, w, b):
    mu = jnp.mean(y, axis=-1, keepdims=True)
    d = y - mu
    var = jnp.mean(d * d, axis=-1, keepdims=True)
    return d * lax.rsqrt(var + LN_EPS) * w + b


def _outproj_kernel(og_ref, os_ref, x_ref, wo_ref, lw_ref, lb_ref, wr_ref, br_ref, x1_ref, xb_ref, lg_ref):
    mix = _dot(og_ref[...].astype(BF16), wo_ref[:GLA_W, :]) + _dot(os_ref[...].astype(BF16), wo_ref[GLA_W:, :])
    x1 = _layer_norm(DN_ALPHA * x_ref[...] + mix, lw_ref[...], lb_ref[...])
    x1_ref[...] = x1
    xb_ref[...] = x1.astype(BF16)
    hi, lo = _split_bf16(x1)
    p = _dot(hi, wr_ref[...])
    q = _dot(lo, wr_ref[:, :R_PAD])
    lg_ref[...] = p[:, :R_PAD] + p[:, R_PAD:] + q + br_ref[...]


def _outproj(o_g, o_s, x, wo, lw, lb, wr, br):
    row = lambda i: (i, 0)
    fix = lambda i: (0, 0)
    return pl.pallas_call(
        _outproj_kernel,
        grid=(N_PAD // OP_TM,),
        in_specs=[
            pl.BlockSpec((OP_TM, GLA_W), row),
            pl.BlockSpec((OP_TM, SB_W), row),
            pl.BlockSpec((OP_TM, D_MODEL), row),
            pl.BlockSpec((D_MODEL, D_MODEL), fix),
            pl.BlockSpec((1, D_MODEL), fix),
            pl.BlockSpec((1, D_MODEL), fix),
            pl.BlockSpec((D_MODEL, 2 * R_PAD), fix),
            pl.BlockSpec((1, R_PAD), fix),
        ],
        out_specs=[
            pl.BlockSpec((OP_TM, D_MODEL), row),
            pl.BlockSpec((OP_TM, D_MODEL), row),
            pl.BlockSpec((OP_TM, R_PAD), row),
        ],
        out_shape=[
            jax.ShapeDtypeStruct((N_PAD, D_MODEL), F32),
            jax.ShapeDtypeStruct((2 * N_PAD, D_MODEL), BF16),
            jax.ShapeDtypeStruct((N_PAD, R_PAD), F32),
        ],
        compiler_params=_cparams(("parallel",)),
        name="outproj_ln1",
    )(o_g, o_s, x, wo, lw, lb, wr, br)


MOE_BM = 256
MOE_CAP = 5
MOE_GR = MOE_BM * MOE_CAP
MOE_TF = 512
N_ASG = N_TOK * TOP_K
MOE_NG = N_ASG // MOE_GR + N_EXP
MOE_NSLOT = MOE_NG * MOE_GR
MOE_NCH = D_EXP // MOE_TF


def _moe_kernel(ge_ref, gn_ref, gx_ref, x_ref, wg_ref, wu_ref, bg_ref, bu_ref, wd_ref, bd_ref, y_ref,
                wgb, wub, wdb, acc):
    del ge_ref, gx_ref
    c = pl.program_id(1)
    n = gn_ref[pl.program_id(0)]
    last = c == pl.num_programs(1) - 1

    @pl.when(n > 0)
    def _():
        wgb[...] = wg_ref[...].astype(BF16)
        wub[...] = wu_ref[...].astype(BF16)
        wdb[...] = wd_ref[...].astype(BF16)

    for i in range(MOE_CAP):
        rows = slice(i * MOE_BM, (i + 1) * MOE_BM)

        @pl.when(jnp.logical_and(i < n, c == 0))
        def _():
            acc[rows, :] = jnp.broadcast_to(bd_ref[...], (MOE_BM, D_MODEL))

        @pl.when(i < n)
        def _():
            x = x_ref[rows, :]
            g = jnp.minimum(_dot(x, wgb[...]) + bg_ref[...], SWIGLU_LIMIT)
            u = jnp.clip(_dot(x, wub[...]) + bu_ref[...], -SWIGLU_LIMIT, SWIGLU_LIMIT)
            act = (u + 1.0) * (g * _sigmoid(SWIGLU_ALPHA * g))
            acc[rows, :] += _dot(act.astype(BF16), wdb[...])

        @pl.when(jnp.logical_and(i < n, last))
        def _():
            y_ref[rows, :] = acc[rows, :].astype(BF16)

        @pl.when(jnp.logical_and(jnp.logical_and(i >= n, n > 0), last))
        def _():
            y_ref[rows, :] = jnp.zeros((MOE_BM, D_MODEL), BF16)


def _moe_experts(g_e, g_n, g_x, xs, w_up, b_up, w_down, b_down, layer):
    chunk = lambda s, c, gn: jnp.where(gn[s] > 0, c, MOE_NCH - 1)
    grid_spec = pltpu.PrefetchScalarGridSpec(
        num_scalar_prefetch=3,
        grid=(MOE_NG, MOE_NCH),
        in_specs=[
            pl.BlockSpec((MOE_GR, D_MODEL), lambda s, c, ge, gn, gx: (gx[s], 0), pipeline_mode=pl.Buffered(1)),
            pl.BlockSpec((None, None, D_MODEL, MOE_TF), lambda s, c, ge, gn, gx: (layer, ge[s], 0, chunk(s, c, gn))),
            pl.BlockSpec((None, None, D_MODEL, MOE_TF),
                         lambda s, c, ge, gn, gx: (layer, ge[s], 0, MOE_NCH + chunk(s, c, gn))),
            pl.BlockSpec((None, None, 1, MOE_TF), lambda s, c, ge, gn, gx: (layer, ge[s], 0, chunk(s, c, gn))),
            pl.BlockSpec((None, None, 1, MOE_TF),
                         lambda s, c, ge, gn, gx: (layer, ge[s], 0, MOE_NCH + chunk(s, c, gn))),
            pl.BlockSpec((None, None, MOE_TF, D_MODEL), lambda s, c, ge, gn, gx: (layer, ge[s], chunk(s, c, gn), 0)),
            pl.BlockSpec((None, None, 1, D_MODEL), lambda s, c, ge, gn, gx: (layer, ge[s], 0, 0)),
        ],
        out_specs=pl.BlockSpec((MOE_GR, D_MODEL), lambda s, c, ge, gn, gx: (gx[s], 0), pipeline_mode=pl.Buffered(1)),
        scratch_shapes=[
            pltpu.VMEM((D_MODEL, MOE_TF), BF16),
            pltpu.VMEM((D_MODEL, MOE_TF), BF16),
            pltpu.VMEM((MOE_TF, D_MODEL), BF16),
            pltpu.VMEM((MOE_GR, D_MODEL), F32),
        ],
    )
    return pl.pallas_call(
        _moe_kernel,
        grid_spec=grid_spec,
        out_shape=jax.ShapeDtypeStruct((MOE_NSLOT, D_MODEL), BF16),
        compiler_params=_cparams(("arbitrary", "arbitrary")),
        name="moe_experts",
    )(g_e, g_n, g_x, xs, w_up, w_up, b_up, b_up, w_down, b_down)


def _route(logits):
    top_val, top_idx = lax.top_k(logits, TOP_K)
    gates = jax.nn.softmax(top_val, axis=-1)
    flat_e = top_idx.reshape(-1).astype(jnp.int32)
    asg = jnp.arange(N_ASG, dtype=jnp.int32)
    _, order = lax.sort((flat_e, asg), num_keys=1)
    _, rank = lax.sort((order, asg), num_keys=1)
    experts = jnp.arange(N_EXP, dtype=jnp.int32)
    counts = jnp.sum((flat_e[:, None] == experts[None, :]).astype(jnp.int32), axis=0)
    starts = jnp.cumsum(counts) - counts
    ng = (counts + MOE_GR - 1) // MOE_GR
    g_end = jnp.cumsum(ng)
    g_first = g_end - ng
    row0 = g_first * MOE_GR
    n_used = g_end[-1]
    s = jnp.arange(MOE_NG, dtype=jnp.int32)
    s_c = jnp.minimum(s, n_used - 1)
    g_e = jnp.sum((g_end[None, :] <= s_c[:, None]).astype(jnp.int32), axis=1)
    left = counts[g_e] - (s_c - g_first[g_e]) * MOE_GR
    g_n = jnp.where(s < n_used, jnp.clip((left + MOE_BM - 1) // MOE_BM, 0, MOE_CAP), 0).astype(jnp.int32)
    shift = row0 - starts
    slot_of_asg = rank + jnp.sum(jnp.where(flat_e[:, None] == experts[None, :], shift[None, :], 0), axis=1)
    r = (s_c - g_first[g_e])[:, None] * MOE_GR + jnp.arange(MOE_GR, dtype=jnp.int32)[None, :]
    valid = jnp.logical_and(r < counts[g_e][:, None], (s < n_used)[:, None])
    slot = s[:, None] * MOE_GR + jnp.arange(MOE_GR, dtype=jnp.int32)[None, :]
    src = jnp.where(valid, starts[g_e][:, None] + r, slot % N_ASG)
    slot_tok = jnp.where(valid, order[src] // TOP_K, slot % N_TOK).astype(jnp.int32).reshape(MOE_NSLOT)
    return gates, slot_tok, slot_of_asg.astype(jnp.int32), g_e.astype(jnp.int32), g_n, s_c.astype(jnp.int32)


LN2_TM = 256


def _ln2_kernel(x_ref, y_ref, g_ref, lw_ref, lb_ref, o_ref):
    g = g_ref[...]
    moe = y_ref[0].astype(F32) * g[:, 0:1]
    for k in range(1, TOP_K):
        moe = moe + y_ref[k].astype(F32) * g[:, k:k + 1]
    o_ref[...] = _layer_norm(DN_ALPHA * x_ref[...] + moe, lw_ref[...], lb_ref[...])


def _ln2(x1, yk, gates, lw, lb):
    return pl.pallas_call(
        _ln2_kernel,
        grid=(N_PAD // LN2_TM,),
        in_specs=[
            pl.BlockSpec((LN2_TM, D_MODEL), lambda i: (i, 0)),
            pl.BlockSpec((TOP_K, LN2_TM, D_MODEL), lambda i: (0, i, 0)),
            pl.BlockSpec((LN2_TM, TOP_K), lambda i: (i, 0)),
            pl.BlockSpec((1, D_MODEL), lambda i: (0, 0)),
            pl.BlockSpec((1, D_MODEL), lambda i: (0, 0)),
        ],
        out_specs=pl.BlockSpec((LN2_TM, D_MODEL), lambda i: (i, 0)),
        out_shape=jax.ShapeDtypeStruct((N_PAD, D_MODEL), F32),
        compiler_params=_cparams(("parallel",)),
        name="combine_ln2",
    )(x1, yk, gates, lw, lb)


def _prep_w_in(w_in, b_sb_qk):
    o_qg, o_kg, o_vg, o_rg, o_a, o_qs = 0, 512, 1024, 2048, 3072, 3088
    o_ks, o_vs = o_qs + SB_W, o_qs + 2 * SB_W
    cols = [
        w_in[:, o_qs:o_qs + SB_W], w_in[:, o_ks:o_ks + SB_W], w_in[:, o_vs:o_vs + SB_W],
        w_in[:, o_qg:o_vg], w_in[:, o_vg:o_rg], w_in[:, o_rg:o_a],
        w_in[:, o_a:o_qs], jnp.zeros((D_MODEL, A_PAD - GLA_RANK), w_in.dtype),
    ]
    w = jnp.concatenate(cols, axis=1).astype(BF16)
    bias = jnp.concatenate([b_sb_qk, jnp.zeros((H_COLS - 2 * SB_W,), F32)]).reshape(1, H_COLS)
    return w, bias


def _prep_router(w_router, b_router):
    wp = jnp.pad(w_router, ((0, 0), (0, R_PAD - N_EXP)))
    hi = wp.astype(BF16)
    lo = (wp - hi.astype(F32)).astype(BF16)
    bp = jnp.pad(b_router, (0, R_PAD - N_EXP)).reshape(1, R_PAD)
    return jnp.concatenate([hi, lo], axis=1), bp


def kernel(x_prompt, x_sample, cache_sb_k, cache_sb_v, state_gla, page_table, w_in, b_sb_qk, w_gla_a2, b_gla_a,
           gla_norm_w, w_o, ln1_w, ln1_b, w_router, b_router, w_up, b_up, w_down, b_down, ln2_w, ln2_b):
    n_pool = cache_sb_k.shape[1]
    cache_k = cache_sb_k.reshape(DEPTH, n_pool, PAGE * SB_H, SB_HD)
    cache_v = cache_sb_v.reshape(DEPTH, n_pool, PAGE * SB_H, SB_HD)
    b_up4 = b_up.reshape(DEPTH, N_EXP, 1, 2 * D_EXP)
    b_down4 = b_down.reshape(DEPTH, N_EXP, 1, D_MODEL)
    x = jnp.concatenate([x_prompt.reshape(N_PROMPT, D_MODEL), x_sample.reshape(N_SAMPLE, D_MODEL),
                         jnp.zeros((N_PAD - N_TOK, D_MODEL), F32)], axis=0)
    kp, vp, gp, ks, vs, gs = [], [], [], [], [], []
    for l in range(DEPTH):
        w1, bias1 = _prep_w_in(w_in[l], b_sb_qk[l])
        hmat = _inproj(x, w1, bias1)
        wa = jnp.pad(w_gla_a2[l], ((0, A_PAD - GLA_RANK), (0, 0))).astype(BF16)
        ba = b_gla_a[l].reshape(1, GLA_QK)
        nw = gla_norm_w[l].reshape(1, GLA_W)
        o_g, g_p = _gla_prompt(hmat, wa, ba, nw)
        o_g, g_s = _gla_sample(hmat, wa, ba, nw, state_gla[l], o_g)
        o_s = _sb_prompt(hmat)
        o_s = _sb_sample(hmat, cache_k, cache_v, page_table, l, o_s)
        wr, br = _prep_router(w_router[l], b_router[l])
        x1, x1b, logits = _outproj(o_g, o_s, x, w_o[l].astype(BF16), ln1_w[l].reshape(1, D_MODEL),
                                   ln1_b[l].reshape(1, D_MODEL), wr, br)
        gates, slot_tok, slot_of_asg, g_e, g_n, g_x = _route(logits[:N_TOK, :N_EXP])
        xs = x1b.at[slot_tok].get(mode="promise_in_bounds")
        ys = _moe_experts(g_e, g_n, g_x, xs, w_up, b_up4, w_down, b_down4, l)
        slot_k = jnp.pad(slot_of_asg.reshape(N_TOK, TOP_K), ((0, N_PAD - N_TOK), (0, 0))).T
        yk = ys.at[slot_k].get(mode="promise_in_bounds")
        gates_p = jnp.pad(gates, ((0, N_PAD - N_TOK), (0, 0)))
        x = _ln2(x1, yk, gates_p, ln2_w[l].reshape(1, D_MODEL), ln2_b[l].reshape(1, D_MODEL))
        kp.append(hmat[:N_PROMPT, C_KS:C_KS + SB_W].reshape(B_P, T_P, SB_H, SB_HD))
        vp.append(hmat[:N_PROMPT, C_VS:C_VS + SB_W].reshape(B_P, T_P, SB_H, SB_HD))
        ks.append(hmat[N_PROMPT:N_TOK, C_KS:C_KS + SB_W].reshape(B_S, T_S, SB_H, SB_HD))
        vs.append(hmat[N_PROMPT:N_TOK, C_VS:C_VS + SB_W].reshape(B_S, T_S, SB_H, SB_HD))
        gp.append(g_p)
        gs.append(g_s)
    return (x[:N_PROMPT].reshape(B_P, T_P, D_MODEL), x[N_PROMPT:N_TOK].reshape(B_S, T_S, D_MODEL),
            jnp.stack(kp), jnp.stack(vp), jnp.stack(gp), jnp.stack(ks), jnp.stack(vs), jnp.stack(gs))
```
